```python
import math
import jax
import jax.numpy as jnp
from jax import lax
import numpy as np

D_MODEL = 1024
BATCH = 32
SEQ = 2048
DEPTH = 4
DEC_BATCH = 32
DEC_SEQ = 16
PAST_LEN = 1024

CHUNK = 64
Q_BLOCK = 128
N_MIXERS = 3
BRANCH = D_MODEL
DA_HEADS = 8
DA_DIM = 64
SB_HEADS = 16
SB_DIM = 64
SW_HEADS = 16
SW_KV_HEADS = 4
SW_GROUP = SW_HEADS // SW_KV_HEADS
SW_DIM = 64
WINDOW = 128
WIN_CHUNKS = WINDOW // CHUNK
N_BUCKETS = 32
MAX_DISTANCE = 128
N_BIAS_HEADS = 16
EPS = 1e-6
NEG_INF = -1e30

DA_QK = DA_HEADS * 2 * DA_DIM
DA_V = DA_HEADS * 2 * DA_DIM
SB_W = SB_HEADS * SB_DIM
SW_Q = SW_HEADS * SW_DIM
SW_KV = SW_KV_HEADS * SW_DIM
IN_WIDTHS = (2 * DA_QK + DA_V + BRANCH, 3 * SB_W + BRANCH, SW_Q + 2 * SW_KV + BRANCH)

kernel_name = 'hybrid_diff_stickbreak_swa_stream_step'


def _rmsnorm(x, g):
    xf = x.astype(jnp.float32)
    y = xf * lax.rsqrt(jnp.mean(xf * xf, axis=-1, keepdims=True) + EPS)
    return (y * g.astype(jnp.float32)).astype(x.dtype)


def _rel_bucket(rel):
    nb = N_BUCKETS // 2
    max_exact = nb // 2
    n = jnp.abs(rel)
    nf = jnp.maximum(n, 1).astype(jnp.float32)
    large = max_exact + (jnp.log(nf / max_exact) / math.log(MAX_DISTANCE / max_exact)
                         * (nb - max_exact)).astype(jnp.int32)
    large = jnp.minimum(large, nb - 1)
    return jnp.where(rel > 0, nb, 0) + jnp.where(n < max_exact, n, large)


def _rel_bias(q_pos, k_pos, table):
    b = _rel_bucket(k_pos[None, :] - q_pos[:, None])
    return jnp.moveaxis(table.astype(jnp.float32)[b], -1, 0)


def _chunk_causal(q_pos, k_pos):
    return (k_pos[None, :] // CHUNK) <= (q_pos[:, None] // CHUNK)


def _sweep_blocks(fn, q, S):
    def blk(i):
        q0 = i * Q_BLOCK
        return fn(lax.dynamic_slice_in_dim(q, q0, Q_BLOCK, axis=1), q0 + jnp.arange(Q_BLOCK))
    o = jnp.moveaxis(lax.map(blk, jnp.arange(S // Q_BLOCK)), 0, 1)
    return o.reshape(o.shape[0], S, -1)


def _split_da(u):
    B, T = u.shape[:2]
    q = u[..., :DA_QK].reshape(B, T, DA_HEADS, 2, DA_DIM)
    k = u[..., DA_QK:2 * DA_QK].reshape(B, T, DA_HEADS, 2, DA_DIM)
    v = u[..., 2 * DA_QK:].reshape(B, T, DA_HEADS, 2 * DA_DIM)
    return q, k, v


def _diff_lambda(lam_params, lam_init):
    lp = lam_params.astype(jnp.float32)
    return jnp.exp(jnp.sum(lp[0] * lp[1])) - jnp.exp(jnp.sum(lp[2] * lp[3])) + lam_init


def _diff_attn(q, k, v, q_pos, k_pos, table, lam, lam_init, subln):
    bias = _rel_bias(q_pos, k_pos, table).reshape(DA_HEADS, 2, q_pos.shape[0], k_pos.shape[0])
    mask = _chunk_causal(q_pos, k_pos)
    s = jnp.einsum('bqhmd,bkhmd->bhmqk', q, k, preferred_element_type=jnp.float32) * (DA_DIM ** -0.5) + bias
    p = jax.nn.softmax(jnp.where(mask, s, NEG_INF), axis=-1)
    w = p[:, :, 0] - lam * p[:, :, 1]
    o = jnp.einsum('bhqk,bkhe->bqhe', w, v.astype(jnp.float32))
    o = o * lax.rsqrt(jnp.mean(o * o, axis=-1, keepdims=True) + EPS) * subln.astype(jnp.float32)
    return o * (1.0 - lam_init)


def _diff_mixer(up, us, ck, cv, table, lam_params, subln, lam_init):
    qp, kp, vp = _split_da(up)
    qs, ks, vs = _split_da(us)
    lam = _diff_lambda(lam_params, lam_init)
    S = up.shape[1]
    pos = jnp.arange(S)
    o_p = _sweep_blocks(lambda qb, qpos: _diff_attn(qb, kp, vp, qpos, pos, table, lam, lam_init, subln), qp, S)
    T = us.shape[1]
    k_all = jnp.concatenate([ck, ks], axis=1)
    v_all = jnp.concatenate([cv, vs], axis=1)
    o_s = _diff_attn(qs, k_all, v_all, PAST_LEN + jnp.arange(T), jnp.arange(PAST_LEN + T),
                     table, lam, lam_init, subln).reshape(us.shape[0], T, BRANCH)
    return o_p, o_s, (kp, vp, ks, vs)


def _split_sb(u):
    B, T = u.shape[:2]
    q = u[..., :SB_W].reshape(B, T, SB_HEADS, SB_DIM)
    k = u[..., SB_W:2 * SB_W].reshape(B, T, SB_HEADS, SB_DIM)
    v = u[..., 2 * SB_W:].reshape(B, T, SB_HEADS, SB_DIM)
    return q, k, v


def _stick_breaking(q, k, v, q_pos, k_pos):
    mask = k_pos[None, :] < q_pos[:, None]
    z = jnp.einsum('bqhd,bkhd->bhqk', q, k, preferred_element_type=jnp.float32) * (SB_DIM ** -0.5)
    log_rest = jnp.where(mask, jax.nn.log_sigmoid(-z), 0.0)
    log_after = lax.cumsum(log_rest, axis=3, reverse=True) - log_rest
    a = jnp.where(mask, jnp.exp(jax.nn.log_sigmoid(z) + log_after), 0.0)
    return jnp.einsum('bhqk,bkhd->bqhd', a, v.astype(jnp.float32))


def _sb_mixer(up, us, ck, cv):
    qp, kp, vp = _split_sb(up)
    qs, ks, vs = _split_sb(us)
    S = up.shape[1]
    pos = jnp.arange(S)
    o_p = _sweep_blocks(lambda qb, qpos: _stick_breaking(qb, kp, vp, qpos, pos), qp, S)
    T = us.shape[1]
    k_all = jnp.concatenate([ck, ks], axis=1)
    v_all = jnp.concatenate([cv, vs], axis=1)
    o_s = _stick_breaking(qs, k_all, v_all, PAST_LEN + jnp.arange(T),
                          jnp.arange(PAST_LEN + T)).reshape(us.shape[0], T, BRANCH)
    return o_p, o_s, (kp, vp, ks, vs)


def _split_sw(u):
    B, T = u.shape[:2]
    q = u[..., :SW_Q].reshape(B, T, SW_KV_HEADS, SW_GROUP, SW_DIM)
    k = u[..., SW_Q:SW_Q + SW_KV].reshape(B, T, SW_KV_HEADS, SW_DIM)
    v = u[..., SW_Q + SW_KV:].reshape(B, T, SW_KV_HEADS, SW_DIM)
    return q, k, v


def _sink_attn(q, k, v, mask, bias, sinks):
    s = jnp.einsum('nqhgd,nkhd->nhgqk', q, k, preferred_element_type=jnp.float32) * (SW_DIM ** -0.5) + bias
    s = jnp.where(mask[:, None, None], s, NEG_INF)
    sk = sinks.astype(jnp.float32)[None, :, :, None, None]
    m = jnp.maximum(jnp.max(s, axis=-1, keepdims=True), sk)
    p = jnp.exp(s - m)
    w = p / (jnp.sum(p, axis=-1, keepdims=True) + jnp.exp(sk - m))
    return jnp.einsum('nhgqk,nkhd->nqhgd', w, v.astype(jnp.float32))


def _sw_mixer(up, us, ck, cv, table, sinks):
    qp, kp, vp = _split_sw(up)
    qs, ks, vs = _split_sw(us)
    sinks = sinks.reshape(SW_KV_HEADS, SW_GROUP)
    B, S = up.shape[:2]
    nc = S // CHUNK
    band = (WIN_CHUNKS + 1) * CHUNK

    def bands(t):
        tc = t.reshape(B, nc, CHUNK, SW_KV_HEADS, SW_DIM)
        tp = jnp.pad(tc, ((0, 0), (WIN_CHUNKS, 0), (0, 0), (0, 0), (0, 0)))
        return jnp.concatenate([tp[:, j:j + nc] for j in range(WIN_CHUNKS + 1)],
                               axis=2).reshape(B * nc, band, SW_KV_HEADS, SW_DIM)

    q_off = jnp.arange(CHUNK)
    k_off = jnp.arange(band) - WIN_CHUNKS * CHUNK
    bias_p = _rel_bias(q_off, k_off, table).reshape(SW_KV_HEADS, SW_GROUP, CHUNK, band)
    k_pos = jnp.arange(nc)[:, None] * CHUNK + k_off[None, :]
    valid = jnp.broadcast_to((k_pos >= 0)[None], (B, nc, band)).reshape(B * nc, 1, band)
    o_p = _sink_attn(qp.reshape(B * nc, CHUNK, SW_KV_HEADS, SW_GROUP, SW_DIM), bands(kp), bands(vp),
                     valid, bias_p, sinks).reshape(B, S, BRANCH)

    T = us.shape[1]
    wb = ck.shape[1]
    k_all = jnp.concatenate([ck, ks], axis=1)
    v_all = jnp.concatenate([cv, vs], axis=1)
    q_pos = PAST_LEN + jnp.arange(T)
    kpos_s = PAST_LEN - wb + jnp.arange(wb + T)
    qc = q_pos[:, None] // CHUNK
    kc = kpos_s[None, :] // CHUNK
    mask_s = ((kc <= qc) & (kc >= qc - WIN_CHUNKS))[None]
    bias_s = _rel_bias(q_pos, kpos_s, table).reshape(SW_KV_HEADS, SW_GROUP, T, wb + T)
    o_s = _sink_attn(qs, k_all, v_all, mask_s, bias_s, sinks).reshape(us.shape[0], T, BRANCH)
    keep = min(WINDOW, S)
    return o_p, o_s, (kp[:, -keep:], vp[:, -keep:], k_all[:, -wb:], v_all[:, -wb:])


def _gated_out(gate, o, w_out):
    return (jax.nn.silu(gate) * o.astype(gate.dtype)) @ w_out


def setup_inputs(seed: int = 0) -> dict:
    keys = iter(jax.random.split(jax.random.key(seed), 64))

    def nrm(shape, scale):
        return scale * jax.random.normal(next(keys), shape, jnp.float32)

    wb = min(WINDOW, PAST_LEN)
    d = {}
    d['x_prompt'] = nrm((BATCH, SEQ, D_MODEL), 1.0)
    d['x_sample'] = nrm((DEC_BATCH, DEC_SEQ, D_MODEL), 1.0)
    for i in range(DEPTH):
        kind = i % N_MIXERS
        if kind == 0:
            ks = (DEC_BATCH, PAST_LEN, DA_HEADS, 2, DA_DIM)
            vs = (DEC_BATCH, PAST_LEN, DA_HEADS, 2 * DA_DIM)
        elif kind == 1:
            ks = vs = (DEC_BATCH, PAST_LEN, SB_HEADS, SB_DIM)
        else:
            ks = vs = (DEC_BATCH, wb, SW_KV_HEADS, SW_DIM)
        d[f'cache_k_{i}'] = nrm(ks, 1.0)
        d[f'cache_v_{i}'] = nrm(vs, 1.0)
    d['rel_bias_table'] = nrm((N_BUCKETS, N_BIAS_HEADS), 0.5)
    for i in range(DEPTH):
        kind = i % N_MIXERS
        d[f'norm_{i}'] = 1.0 + nrm((D_MODEL,), 0.02)
        d[f'w_in_{i}'] = nrm((D_MODEL, IN_WIDTHS[kind]), D_MODEL ** -0.5)
        d[f'w_out_{i}'] = nrm((BRANCH, D_MODEL), BRANCH ** -0.5)
        if kind == 0:
            d[f'da_lambda_{i}'] = nrm((4, DA_DIM), 0.1)
            d[f'da_subln_{i}'] = 1.0 + nrm((2 * DA_DIM,), 0.02)
        elif kind == 2:
            d[f'sw_sinks_{i}'] = nrm((SW_HEADS,), 1.0)
    d['final_norm'] = 1.0 + nrm((D_MODEL,), 0.02)
    return d


def reference(x_prompt, x_sample, cache_k_0, cache_v_0, cache_k_1, cache_v_1, cache_k_2, cache_v_2,
              cache_k_3, cache_v_3, rel_bias_table,
              norm_0, w_in_0, w_out_0, da_lambda_0, da_subln_0,
              norm_1, w_in_1, w_out_1,
              norm_2, w_in_2, w_out_2, sw_sinks_2,
              norm_3, w_in_3, w_out_3, da_lambda_3, da_subln_3,
              final_norm):
    caches = [(cache_k_0, cache_v_0), (cache_k_1, cache_v_1), (cache_k_2, cache_v_2), (cache_k_3, cache_v_3)]
    layers = [
        dict(norm=norm_0, w_in=w_in_0, w_out=w_out_0, lam=da_lambda_0, subln=da_subln_0),
        dict(norm=norm_1, w_in=w_in_1, w_out=w_out_1),
        dict(norm=norm_2, w_in=w_in_2, w_out=w_out_2, sinks=sw_sinks_2),
        dict(norm=norm_3, w_in=w_in_3, w_out=w_out_3, lam=da_lambda_3, subln=da_subln_3),
    ]
    xp, xs = x_prompt, x_sample
    states = []
    for i in range(DEPTH):
        p = layers[i]
        ck, cv = caches[i]
        kind = i % N_MIXERS
        hp = _rmsnorm(xp, p['norm']) @ p['w_in']
        hs = _rmsnorm(xs, p['norm']) @ p['w_in']
        up, gp = hp[..., :-BRANCH], hp[..., -BRANCH:]
        us, gs = hs[..., :-BRANCH], hs[..., -BRANCH:]
        if kind == 0:
            lam_init = 0.8 - 0.6 * math.exp(-0.3 * i)
            o_p, o_s, st = _diff_mixer(up, us, ck, cv, rel_bias_table, p['lam'], p['subln'], lam_init)
        elif kind == 1:
            o_p, o_s, st = _sb_mixer(up, us, ck, cv)
        else:
            o_p, o_s, st = _sw_mixer(up, us, ck, cv, rel_bias_table, p['sinks'])
        xp = xp + _gated_out(gp, o_p, p['w_out'])
        xs = xs + _gated_out(gs, o_s, p['w_out'])
        states.append(st)
    y_prompt = _rmsnorm(xp, final_norm)
    y_sample = _rmsnorm(xs, final_norm)
    return (y_prompt, y_sample,
            states[0][0], states[0][1], states[0][2], states[0][3],
            states[1][0], states[1][1], states[1][2], states[1][3],
            states[2][0], states[2][1], states[2][2], states[2][3],
            states[3][0], states[3][1], states[3][2], states[3][3])
```

```python
import functools
import math

import jax
import jax.numpy as jnp
import numpy as np
from jax import lax
from jax.experimental import pallas as pl
from jax.experimental.pallas import tpu as pltpu

F32 = jnp.float32
BF16 = jnp.bfloat16

EPS = 1e-6
NEG_INF = -1e30
CHUNK = 64
WINDOW = 128
WIN_CHUNKS = WINDOW // CHUNK
N_BUCKETS = 32
MAX_DISTANCE = 128
FAR_BUCKET = N_BUCKETS // 2 - 1
HEAD_SLAB = 128
HALF = HEAD_SLAB // 2
VMEM_LIMIT = 48 * 1024 * 1024

_NT = (((1,), (1,)), ((), ()))


def _cparams(sem):
    return pltpu.CompilerParams(dimension_semantics=sem, vmem_limit_bytes=VMEM_LIMIT)


def _inproj_kernel(x_ref, g_ref, w_ref, *out_refs, plan):
    x = x_ref[...]
    y = x * lax.rsqrt(jnp.mean(x * x, axis=-1, keepdims=True) + EPS)
    xn = (y * g_ref[...]).astype(BF16)
    h = jnp.dot(xn, w_ref[...], preferred_element_type=F32)
    for o_ref, (c0, width, kind) in zip(out_refs, plan):
        t = h[:, c0:c0 + width]
        if kind == "qscale":
            t = t * (HALF ** -0.5)
        elif kind == "silu":
            t = t * jax.nn.sigmoid(t)
        o_ref[...] = t.astype(o_ref.dtype)


def _inproj(x2d, g, w_bf16, plan, dtypes, tm):
    t, d = x2d.shape
    n = w_bf16.shape[1]
    tm = min(tm, t)
    assert t % tm == 0
    out_shape = [jax.ShapeDtypeStruct((t, width), dt) for (_, width, _), dt in zip(plan, dtypes)]
    out_specs = [pl.BlockSpec((tm, width), lambda i: (i, 0)) for (_, width, _) in plan]
    return pl.pallas_call(
        functools.partial(_inproj_kernel, plan=tuple(plan)),
        grid=(t // tm,),
        in_specs=[
            pl.BlockSpec((tm, d), lambda i: (i, 0)),
            pl.BlockSpec((1, d), lambda i: (0, 0)),
            pl.BlockSpec((d, n), lambda i: (0, 0)),
        ],
        out_specs=out_specs,
        out_shape=out_shape,
        compiler_params=_cparams(("parallel",)),
        name="inproj",
    )(x2d, g.reshape(1, d), w_bf16)


def _outproj_kernel(x_ref, og_ref, w_ref, *rest, final):
    xn = x_ref[...] + jnp.dot(og_ref[...], w_ref[...], preferred_element_type=F32)
    if final:
        g_ref, o_ref = rest
        y = xn * lax.rsqrt(jnp.mean(xn * xn, axis=-1, keepdims=True) + EPS)
        o_ref[...] = y * g_ref[...]
    else:
        (o_ref,) = rest
        o_ref[...] = xn


def _outproj(x2d, og, w_bf16, final_g, tm):
    t, d = x2d.shape
    br = og.shape[1]
    tm = min(tm, t)
    assert t % tm == 0
    in_specs = [
        pl.BlockSpec((tm, d), lambda i: (i, 0)),
        pl.BlockSpec((tm, br), lambda i: (i, 0)),
        pl.BlockSpec((br, d), lambda i: (0, 0)),
    ]
    args = [x2d, og, w_bf16]
    if final_g is not None:
        in_specs.append(pl.BlockSpec((1, d), lambda i: (0, 0)))
        args.append(final_g.reshape(1, d))
    return pl.pallas_call(
        functools.partial(_outproj_kernel, final=final_g is not None),
        grid=(t // tm,),
        in_specs=in_specs,
        out_specs=pl.BlockSpec((tm, d), lambda i: (i, 0)),
        out_shape=jax.ShapeDtypeStruct((t, d), F32),
        compiler_params=_cparams(("parallel",)),
        name="outproj",
    )(*args)


def _rel_bucket(rel):
    nb = N_BUCKETS // 2
    max_exact = nb // 2
    n = jnp.abs(rel)
    nf = jnp.maximum(n, 1).astype(F32)
    large = max_exact + (jnp.log(nf / max_exact) / math.log(MAX_DISTANCE / max_exact)
                         * (nb - max_exact)).astype(jnp.int32)
    large = jnp.minimum(large, nb - 1)
    return jnp.where(rel > 0, nb, 0) + jnp.where(n < max_exact, n, large)


def _bias_kernel(tab_ref, idx_ref, o_ref):
    h = pl.program_id(0)
    idx = idx_ref[...]
    acc = jnp.zeros(idx.shape, F32)
    for b in range(N_BUCKETS):
        acc = jnp.where(idx == b, tab_ref[b, h], acc)
    o_ref[0] = acc


def _bias_tiles(table, rel):
    idx = _rel_bucket(rel).astype(jnp.int32)
    r, c = idx.shape
    nh = table.shape[1]
    return pl.pallas_call(
        _bias_kernel,
        grid=(nh,),
        in_specs=[
            pl.BlockSpec(memory_space=pltpu.SMEM),
            pl.BlockSpec((r, c), lambda h: (0, 0)),
        ],
        out_specs=pl.BlockSpec((1, r, c), lambda h: (h, 0, 0)),
        out_shape=jax.ShapeDtypeStruct((nh, r, c), F32),
        compiler_params=_cparams(("arbitrary",)),
        name="bias_tiles",
    )(table, idx)


def _split_halves(q_bf16):
    qf = q_bf16.astype(F32)
    lane = lax.broadcasted_iota(jnp.int32, qf.shape, 1)
    qa = jnp.where(lane < HALF, qf, 0.0).astype(BF16)
    qb = jnp.where(lane >= HALF, qf, 0.0).astype(BF16)
    return jnp.concatenate([qa, qb], axis=0)


def _join_halves(o, t):
    lane = lax.broadcasted_iota(jnp.int32, (t, HEAD_SLAB), 1)
    return jnp.where(lane < HALF, o[:t], o[t:])


def _softmax_step(s, v_bf16, carry):
    m, l, acc = carry
    m_new = jnp.maximum(m, jnp.max(s, axis=-1, keepdims=True))
    alpha = jnp.exp(m - m_new)
    p = jnp.exp(s - m_new)
    l = alpha * l + jnp.sum(p, axis=-1, keepdims=True)
    acc = alpha * acc + jnp.dot(p.astype(BF16), v_bf16, preferred_element_type=F32)
    return m_new, l, acc


def _da_finish(carry, t, lam, lam_init, subln, sg):
    _, l, acc = carry
    o = acc[:t] / l[:t] - lam * (acc[t:] / l[t:])
    o = o * lax.rsqrt(jnp.mean(o * o, axis=-1, keepdims=True) + EPS) * subln
    o = o * (1.0 - lam_init)
    return (sg.astype(F32) * o).astype(BF16)


def _lam_kernel(lp_ref, o_ref, *, lam_init):
    lp = lp_ref[...]
    a = jnp.sum(lp[0:1] * lp[1:2], axis=-1, keepdims=True)
    b = jnp.sum(lp[2:3] * lp[3:4], axis=-1, keepdims=True)
    o_ref[...] = jnp.exp(a) - jnp.exp(b) + lam_init


def _diff_lambda(lam_params, lam_init):
    return pl.pallas_call(
        functools.partial(_lam_kernel, lam_init=lam_init),
        out_shape=jax.ShapeDtypeStruct((1, 1), F32),
        name="diff_lambda",
    )(lam_params)


def _da_prompt_kernel(tab_ref, lam_ref, q_ref, k_ref, v_ref, sg_ref, bias_ref, subln_ref, o_ref,
                      *, tq, lam_init):
    h = pl.program_id(1)
    qi = pl.program_id(2)
    qq = _split_halves(q_ref[...])
    row = lax.broadcasted_iota(jnp.int32, (2 * tq, 1), 0)
    far = jnp.where(row < tq, tab_ref[FAR_BUCKET, 2 * h], tab_ref[FAR_BUCKET, 2 * h + 1])

    def block(j, bias, mask, carry):
        start = pl.multiple_of(j * tq, tq)
        kj = k_ref[pl.ds(start, tq), :]
        vj = v_ref[pl.ds(start, tq), :]
        s = lax.dot_general(qq, kj, _NT, preferred_element_type=F32) + bias
        if mask is not None:
            s = jnp.where(mask, s, NEG_INF)
        return _softmax_step(s, vj, carry)

    carry = (jnp.full((2 * tq, 1), NEG_INF, F32), jnp.zeros((2 * tq, 1), F32),
             jnp.zeros((2 * tq, HEAD_SLAB), F32))
    r = lax.broadcasted_iota(jnp.int32, (2 * tq, tq), 0) % tq
    c = lax.broadcasted_iota(jnp.int32, (2 * tq, tq), 1)
    chunk_causal = (c // CHUNK) <= (r // CHUNK)
    carry = block(qi, bias_ref[:, 0].reshape(2 * tq, tq), chunk_causal, carry)
    carry = block(jnp.maximum(qi - 1, 0), bias_ref[:, 1].reshape(2 * tq, tq), qi >= 1, carry)
    carry = lax.fori_loop(0, jnp.maximum(qi - 1, 0), lambda j, cr: block(j, far, None, cr), carry)
    o_ref[...] = _da_finish(carry, tq, lam_ref[0, 0], lam_init, subln_ref[...], sg_ref[...])


def _da_prompt(q, k, v, sg, bias, table, lam, subln, batch, seq, tq, lam_init):
    t, width = q.shape
    nh = width // HEAD_SLAB
    nq = seq // tq
    return pl.pallas_call(
        functools.partial(_da_prompt_kernel, tq=tq, lam_init=lam_init),
        grid=(batch, nh, nq),
        in_specs=[
            pl.BlockSpec(memory_space=pltpu.SMEM),
            pl.BlockSpec(memory_space=pltpu.SMEM),
            pl.BlockSpec((tq, HEAD_SLAB), lambda b, h, i: (b * nq + i, h)),
            pl.BlockSpec((seq, HEAD_SLAB), lambda b, h, i: (b, h)),
            pl.BlockSpec((seq, HEAD_SLAB), lambda b, h, i: (b, h)),
            pl.BlockSpec((tq, HEAD_SLAB), lambda b, h, i: (b * nq + i, h)),
            pl.BlockSpec((2, 2, tq, tq), lambda b, h, i: (h, 0, 0, 0)),
            pl.BlockSpec((1, HEAD_SLAB), lambda b, h, i: (0, 0)),
        ],
        out_specs=pl.BlockSpec((tq, HEAD_SLAB), lambda b, h, i: (b * nq + i, h)),
        out_shape=jax.ShapeDtypeStruct((t, width), BF16),
        compiler_params=_cparams(("parallel", "parallel", "arbitrary")),
        name="da_prompt",
    )(table, lam, q, k, v, sg, bias, subln.reshape(1, HEAD_SLAB))


def _da_sample_kernel(lam_ref, q_ref, kc_ref, vc_ref, kn_ref, vn_ref, sg_ref, bc_ref, bn_ref,
                      subln_ref, o_ref, *, t, past, lam_init):
    qq = _split_halves(q_ref[...])
    qpos = past + lax.broadcasted_iota(jnp.int32, (2 * t, 1), 0) % t
    carry = (jnp.full((2 * t, 1), NEG_INF, F32), jnp.zeros((2 * t, 1), F32),
             jnp.zeros((2 * t, HEAD_SLAB), F32))
    kpos_n = past + lax.broadcasted_iota(jnp.int32, (1, t), 1)
    s = lax.dot_general(qq, kn_ref[...], _NT, preferred_element_type=F32) + bn_ref[...].reshape(2 * t, t)
    s = jnp.where((kpos_n // CHUNK) <= (qpos // CHUNK), s, NEG_INF)
    carry = _softmax_step(s, vn_ref[...], carry)
    kpos_c = lax.broadcasted_iota(jnp.int32, (1, past), 1)
    s = lax.dot_general(qq, kc_ref[0].astype(BF16), _NT, preferred_element_type=F32)
    s = s + bc_ref[...].reshape(2 * t, past)
    s = jnp.where((kpos_c // CHUNK) <= (qpos // CHUNK), s, NEG_INF)
    carry = _softmax_step(s, vc_ref[0].astype(BF16), carry)
    o_ref[...] = _da_finish(carry, t, lam_ref[0, 0], lam_init, subln_ref[...], sg_ref[...])


def _da_sample(q, ck, cv, kn, vn, sg, bias_c, bias_n, lam, subln, batch, t, lam_init):
    tt, width = q.shape
    nh = width // HEAD_SLAB
    past = ck.shape[1]
    return pl.pallas_call(
        functools.partial(_da_sample_kernel, t=t, past=past, lam_init=lam_init),
        grid=(batch, nh),
        in_specs=[
            pl.BlockSpec(memory_space=pltpu.SMEM),
            pl.BlockSpec((t, HEAD_SLAB), lambda b, h: (b, h)),
            pl.BlockSpec((1, past, HEAD_SLAB), lambda b, h: (b, 0, h)),
            pl.BlockSpec((1, past, HEAD_SLAB), lambda b, h: (b, 0, h)),
            pl.BlockSpec((t, HEAD_SLAB), lambda b, h: (b, h)),
            pl.BlockSpec((t, HEAD_SLAB), lambda b, h: (b, h)),
            pl.BlockSpec((t, HEAD_SLAB), lambda b, h: (b, h)),
            pl.BlockSpec((2, t, past), lambda b, h: (h, 0, 0)),
            pl.BlockSpec((2, t, t), lambda b, h: (h, 0, 0)),
            pl.BlockSpec((1, HEAD_SLAB), lambda b, h: (0, 0)),
        ],
        out_specs=pl.BlockSpec((t, HEAD_SLAB), lambda b, h: (b, h)),
        out_shape=jax.ShapeDtypeStruct((tt, width), BF16),
        compiler_params=_cparams(("parallel", "parallel")),
        name="da_sample",
    )(lam, q, ck, cv, kn, vn, sg, bias_c, bias_n, subln.reshape(1, HEAD_SLAB))


def _suffix_matrix(n):
    j = np.arange(n)[:, None]
    s = np.arange(n)[None, :]
    return jnp.asarray((j > s).astype(np.float32), dtype=BF16)


def _sb_step(z, v_bf16, tmat, mask, carry):
    c, acc = carry
    lp = jnp.log1p(jnp.exp(-jnp.abs(z)))
    ls = jnp.minimum(z, 0.0) - lp
    lr = ls - z
    if mask is not None:
        lr = jnp.where(mask, lr, 0.0)
    hi = lr.astype(BF16)
    lo = (lr - hi.astype(F32)).astype(BF16)
    after = (jnp.dot(hi, tmat, preferred_element_type=F32)
             + jnp.dot(lo, tmat, preferred_element_type=F32))
    a = jnp.exp(ls + after + c)
    if mask is not None:
        a = jnp.where(mask, a, 0.0)
    acc = acc + jnp.dot(a.astype(BF16), v_bf16, preferred_element_type=F32)
    c = c + jnp.sum(lr, axis=-1, keepdims=True)
    return c, acc


def _sb_prompt_kernel(q_ref, k_ref, v_ref, sg_ref, tmat_ref, o_ref, *, tq):
    qi = pl.program_id(2)
    qq = _split_halves(q_ref[...])
    tmat = tmat_ref[...]

    def block(j, mask, carry):
        start = pl.multiple_of(j * tq, tq)
        kj = k_ref[pl.ds(start, tq), :]
        vj = v_ref[pl.ds(start, tq), :]
        z = lax.dot_general(qq, kj, _NT, preferred_element_type=F32)
        return _sb_step(z, vj, tmat, mask, carry)

    carry = (jnp.zeros((2 * tq, 1), F32), jnp.zeros((2 * tq, HEAD_SLAB), F32))
    r = lax.broadcasted_iota(jnp.int32, (2 * tq, tq), 0) % tq
    c = lax.broadcasted_iota(jnp.int32, (2 * tq, tq), 1)
    carry = block(qi, c < r, carry)
    carry = lax.fori_loop(0, qi, lambda i, cr: block(qi - 1 - i, None, cr), carry)
    o = _join_halves(carry[1], tq)
    o_ref[...] = (sg_ref[...].astype(F32) * o).astype(BF16)


def _sb_prompt(q, k, v, sg, batch, seq, tq):
    t, width = q.shape
    nh = width // HEAD_SLAB
    nq = seq // tq
    return pl.pallas_call(
        functools.partial(_sb_prompt_kernel, tq=tq),
        grid=(batch, nh, nq),
        in_specs=[
            pl.BlockSpec((tq, HEAD_SLAB), lambda b, h, i: (b * nq + i, h)),
            pl.BlockSpec((seq, HEAD_SLAB), lambda b, h, i: (b, h)),
            pl.BlockSpec((seq, HEAD_SLAB), lambda b, h, i: (b, h)),
            pl.BlockSpec((tq, HEAD_SLAB), lambda b, h, i: (b * nq + i, h)),
            pl.BlockSpec((tq, tq), lambda b, h, i: (0, 0)),
        ],
        out_specs=pl.BlockSpec((tq, HEAD_SLAB), lambda b, h, i: (b * nq + i, h)),
        out_shape=jax.ShapeDtypeStruct((t, width), BF16),
        compiler_params=_cparams(("parallel", "parallel", "arbitrary")),
        name="sb_prompt",
    )(q, k, v, sg, _suffix_matrix(tq))


def _sb_sample_kernel(q_ref, kc_ref, vc_ref, kn_ref, vn_ref, sg_ref, tmat_ref, o_ref, *, t, tn, tk, past):
    qq = _split_halves(q_ref[...])
    tmat = tmat_ref[...]
    carry = (jnp.zeros((2 * t, 1), F32), jnp.zeros((2 * t, HEAD_SLAB), F32))
    pad = jnp.zeros((tn - t, HEAD_SLAB), BF16)
    kn = jnp.concatenate([kn_ref[...], pad], axis=0)
    vn = jnp.concatenate([vn_ref[...], pad], axis=0)
    r = lax.broadcasted_iota(jnp.int32, (2 * t, tn), 0) % t
    c = lax.broadcasted_iota(jnp.int32, (2 * t, tn), 1)
    z = lax.dot_general(qq, kn, _NT, preferred_element_type=F32)
    carry = _sb_step(z, vn, tmat[:tn, :tn], c < r, carry)
    for j in reversed(range(past // tk)):
        kj = kc_ref[0, j * tk:(j + 1) * tk, :].astype(BF16)
        vj = vc_ref[0, j * tk:(j + 1) * tk, :].astype(BF16)
        z = lax.dot_general(qq, kj, _NT, preferred_element_type=F32)
        carry = _sb_step(z, vj, tmat, None, carry)
    o = _join_halves(carry[1], t)
    o_ref[...] = (sg_ref[...].astype(F32) * o).astype(BF16)


def _sb_sample(q, ck, cv, kn, vn, sg, batch, t, tk):
    tt, width = q.shape
    nh = width // HEAD_SLAB
    past = ck.shape[1]
    tk = min(tk, past)
    assert past % tk == 0
    tn = HEAD_SLAB
    assert t <= tn <= tk
    return pl.pallas_call(
        functools.partial(_sb_sample_kernel, t=t, tn=tn, tk=tk, past=past),
        grid=(batch, nh),
        in_specs=[
            pl.BlockSpec((t, HEAD_SLAB), lambda b, h: (b, h)),
            pl.BlockSpec((1, past, HEAD_SLAB), lambda b, h: (b, 0, h)),
            pl.BlockSpec((1, past, HEAD_SLAB), lambda b, h: (b, 0, h)),
            pl.BlockSpec((t, HEAD_SLAB), lambda b, h: (b, h)),
            pl.BlockSpec((t, HEAD_SLAB), lambda b, h: (b, h)),
            pl.BlockSpec((t, HEAD_SLAB), lambda b, h: (b, h)),
            pl.BlockSpec((tk, tk), lambda b, h: (0, 0)),
        ],
        out_specs=pl.BlockSpec((t, HEAD_SLAB), lambda b, h: (b, h)),
        out_shape=jax.ShapeDtypeStruct((tt, width), BF16),
        compiler_params=_cparams(("parallel", "parallel")),
        name="sb_sample",
    )(q, ck, cv, kn, vn, sg, _suffix_matrix(tk))


def _sw_group(sink_ref, q_slabs, kk, vv, bias, valid, kvh, group, t):
    qst = jnp.concatenate([_split_halves(qs) for qs in q_slabs], axis=0)
    s = lax.dot_general(qst, kk, _NT, preferred_element_type=F32) + bias
    s = jnp.where(valid, s, NEG_INF)
    row = lax.broadcasted_iota(jnp.int32, (group * t, 1), 0)
    sk = jnp.zeros((group * t, 1), F32)
    for g in range(group):
        sk = jnp.where(row // t == g, sink_ref[kvh * group + g], sk)
    m = jnp.maximum(jnp.max(s, axis=-1, keepdims=True), sk)
    p = jnp.exp(s - m)
    w = p / (jnp.sum(p, axis=-1, keepdims=True) + jnp.exp(sk - m))
    o = jnp.dot(w.astype(BF16), vv, preferred_element_type=F32)
    return [_join_halves(o[2 * i * t:(2 * i + 2) * t], t) for i in range(group // 2)]


def _sw_prompt_kernel(sink_ref, q_ref, kp_ref, kc_ref, vp_ref, vc_ref, sg_ref, bias_ref, o_ref,
                      *, tq, n_kv, group):
    qi = pl.program_id(1)
    band = (WIN_CHUNKS + 1) * CHUNK
    kcat = jnp.concatenate([kp_ref[...], kc_ref[...]], axis=0)
    vcat = jnp.concatenate([vp_ref[...], vc_ref[...]], axis=0)
    col = lax.broadcasted_iota(jnp.int32, (1, band), 1)
    for cc in range(tq // CHUNK):
        r0 = cc * CHUNK
        valid = (col >= WINDOW - r0) | (qi > 0)
        for kvh in range(n_kv):
            kk = kcat[r0:r0 + band, kvh * HEAD_SLAB:(kvh + 1) * HEAD_SLAB]
            vv = vcat[r0:r0 + band, kvh * HEAD_SLAB:(kvh + 1) * HEAD_SLAB]
            s0 = kvh * (group // 2)
            q_slabs = [q_ref[r0:r0 + CHUNK, (s0 + i) * HEAD_SLAB:(s0 + i + 1) * HEAD_SLAB]
                       for i in range(group // 2)]
            bias = bias_ref[kvh * group:(kvh + 1) * group].reshape(group * CHUNK, band)
            outs = _sw_group(sink_ref, q_slabs, kk, vv, bias, valid, kvh, group, CHUNK)
            for i, o in enumerate(outs):
                lanes = slice((s0 + i) * HEAD_SLAB, (s0 + i + 1) * HEAD_SLAB)
                sg = sg_ref[r0:r0 + CHUNK, lanes].astype(F32)
                o_ref[r0:r0 + CHUNK, lanes] = (sg * o).astype(BF16)


def _sw_prompt(q, kd, vd, sg, bias, sinks, batch, seq, tq, n_kv, group):
    t, width = q.shape
    nq = seq // tq
    wpb = tq // WINDOW
    kvw = kd.shape[1]
    band = (WIN_CHUNKS + 1) * CHUNK
    prev = lambda b, i: (jnp.maximum((b * nq + i) * wpb - 1, 0), 0)
    cur = lambda b, i: (b * nq + i, 0)
    return pl.pallas_call(
        functools.partial(_sw_prompt_kernel, tq=tq, n_kv=n_kv, group=group),
        grid=(batch, nq),
        in_specs=[
            pl.BlockSpec(memory_space=pltpu.SMEM),
            pl.BlockSpec((tq, width), cur),
            pl.BlockSpec((WINDOW, kvw), prev),
            pl.BlockSpec((tq, kvw), cur),
            pl.BlockSpec((WINDOW, kvw), prev),
            pl.BlockSpec((tq, kvw), cur),
            pl.BlockSpec((tq, width), cur),
            pl.BlockSpec((n_kv * group, CHUNK, band), lambda b, i: (0, 0, 0)),
        ],
        out_specs=pl.BlockSpec((tq, width), cur),
        out_shape=jax.ShapeDtypeStruct((t, width), BF16),
        compiler_params=_cparams(("parallel", "arbitrary")),
        name="sw_prompt",
    )(sinks, q, kd, kd, vd, vd, sg, bias)


def _sw_sample_kernel(sink_ref, q_ref, k_ref, v_ref, sg_ref, bias_ref, o_ref, *, t, past, wb, n_kv, group):
    band = wb + t
    qpos = past + lax.broadcasted_iota(jnp.int32, (group * t, 1), 0) % t
    kpos = past - wb + lax.broadcasted_iota(jnp.int32, (1, band), 1)
    qc = qpos // CHUNK
    kc = kpos // CHUNK
    valid = (kc <= qc) & (kc >= qc - WIN_CHUNKS)
    for kvh in range(n_kv):
        kk = k_ref[0, :, kvh * HEAD_SLAB:(kvh + 1) * HEAD_SLAB]
        vv = v_ref[0, :, kvh * HEAD_SLAB:(kvh + 1) * HEAD_SLAB]
        s0 = kvh * (group // 2)
        q_slabs = [q_ref[:, (s0 + i) * HEAD_SLAB:(s0 + i + 1) * HEAD_SLAB] for i in range(group // 2)]
        bias = bias_ref[kvh * group:(kvh + 1) * group].reshape(group * t, band)
        outs = _sw_group(sink_ref, q_slabs, kk, vv, bias, valid, kvh, group, t)
        for i, o in enumerate(outs):
            lanes = slice((s0 + i) * HEAD_SLAB, (s0 + i + 1) * HEAD_SLAB)
            o_ref[:, lanes] = (sg_ref[:, lanes].astype(F32) * o).astype(BF16)


def _sw_sample(q, kd, vd, sg, bias, sinks, batch, t, past, wb, n_kv, group):
    tt, width = q.shape
    band, kvw = kd.shape[1:]
    return pl.pallas_call(
        functools.partial(_sw_sample_kernel, t=t, past=past, wb=wb, n_kv=n_kv, group=group),
        grid=(batch,),
        in_specs=[
            pl.BlockSpec(memory_space=pltpu.SMEM),
            pl.BlockSpec((t, width), lambda b: (b, 0)),
            pl.BlockSpec((1, band, kvw), lambda b: (b, 0, 0)),
            pl.BlockSpec((1, band, kvw), lambda b: (b, 0, 0)),
            pl.BlockSpec((t, width), lambda b: (b, 0)),
            pl.BlockSpec((n_kv * group, t, band), lambda b: (0, 0, 0)),
        ],
        out_specs=pl.BlockSpec((t, width), lambda b: (b, 0)),
        out_shape=jax.ShapeDtypeStruct((tt, width), BF16),
        compiler_params=_cparams(("parallel",)),
        name="sw_sample",
    )(sinks, q, kd, vd, sg, bias)


TM_PROJ = 256
TQ_ATTN = 256


def _da_layer(xp, xs, ck, cv, table, norm, w_in, lam_params, subln, lam_init, dims):
    batch, seq, dec_batch, t_dec, past, d = dims
    br = d
    plan = [(0, br, "qscale"), (br, br, "none"), (2 * br, br, "none"), (3 * br, br, "silu"),
            (br, br, "none"), (2 * br, br, "none")]
    dtypes = [BF16, BF16, BF16, BF16, F32, F32]
    w = w_in.astype(BF16)
    qp, kp, vp, sgp, kpf, vpf = _inproj(xp, norm, w, plan, dtypes, TM_PROJ)
    qs, ks, vs, sgs, ksf, vsf = _inproj(xs, norm, w, plan, dtypes, TM_PROJ)
    lam = _diff_lambda(lam_params, lam_init)
    tq = min(TQ_ATTN, seq)
    assert seq % tq == 0 and tq >= MAX_DISTANCE and tq % CHUNK == 0
    assert past == ck.shape[1] and past % CHUNK == 0 and t_dec <= CHUNK
    i = jnp.arange(tq)[:, None]
    j = jnp.arange(tq)[None, :]
    rel_p = jnp.concatenate([j - i, j - tq - i], axis=0)
    nbh = table.shape[1]
    bias_p = _bias_tiles(table, rel_p).reshape(nbh, 2, tq, tq)
    og_p = _da_prompt(qp, kp, vp, sgp, bias_p, table, lam, subln, batch, seq, tq, lam_init)
    q_pos = past + jnp.arange(t_dec)[:, None]
    bias_c = _bias_tiles(table, jnp.arange(past)[None, :] - q_pos)
    bias_n = _bias_tiles(table, past + jnp.arange(t_dec)[None, :] - q_pos)
    og_s = _da_sample(qs, ck.reshape(dec_batch, past, br), cv.reshape(dec_batch, past, br), ks, vs, sgs,
                      bias_c, bias_n, lam, subln, dec_batch, t_dec, lam_init)
    state = (kpf.reshape((batch, seq) + ck.shape[2:]), vpf.reshape((batch, seq) + cv.shape[2:]),
             ksf.reshape((dec_batch, t_dec) + ck.shape[2:]), vsf.reshape((dec_batch, t_dec) + cv.shape[2:]))
    return og_p, og_s, state


def _sb_layer(xp, xs, ck, cv, norm, w_in, dims):
    batch, seq, dec_batch, t_dec, past, d = dims
    br = d
    plan = [(0, br, "qscale"), (br, br, "none"), (2 * br, br, "none"), (3 * br, br, "silu"),
            (br, br, "none"), (2 * br, br, "none")]
    dtypes = [BF16, BF16, BF16, BF16, F32, F32]
    w = w_in.astype(BF16)
    qp, kp, vp, sgp, kpf, vpf = _inproj(xp, norm, w, plan, dtypes, TM_PROJ)
    qs, ks, vs, sgs, ksf, vsf = _inproj(xs, norm, w, plan, dtypes, TM_PROJ)
    tq = min(TQ_ATTN, seq)
    assert seq % tq == 0 and past == ck.shape[1]
    og_p = _sb_prompt(qp, kp, vp, sgp, batch, seq, tq)
    og_s = _sb_sample(qs, ck.reshape(dec_batch, past, br), cv.reshape(dec_batch, past, br), ks, vs, sgs,
                      dec_batch, t_dec, TQ_ATTN)
    state = (kpf.reshape((batch, seq) + ck.shape[2:]), vpf.reshape((batch, seq) + cv.shape[2:]),
             ksf.reshape((dec_batch, t_dec) + ck.shape[2:]), vsf.reshape((dec_batch, t_dec) + cv.shape[2:]))
    return og_p, og_s, state


def _dup_heads(a, n_kv, hd):
    lead = a.shape[:-1]
    a = a.reshape(lead + (n_kv, 1, hd))
    return jnp.broadcast_to(a, lead + (n_kv, 2, hd)).reshape(lead + (n_kv * 2 * hd,))


def _sw_layer(xp, xs, ck, cv, table, norm, w_in, sinks, dims):
    batch, seq, dec_batch, t_dec, past, d = dims
    br = d
    wb, n_kv, hd = ck.shape[1:]
    assert hd == HALF
    kvw = n_kv * hd
    group = br // hd // n_kv
    assert group % 2 == 0
    wq, wk, wv, wg = (w_in[:, :br], w_in[:, br:br + kvw], w_in[:, br + kvw:br + 2 * kvw],
                      w_in[:, br + 2 * kvw:])
    w_p = jnp.concatenate([wq, _dup_heads(wk, n_kv, hd), _dup_heads(wv, n_kv, hd), wg], axis=1).astype(BF16)
    plan_p = [(0, br, "qscale"), (br, 2 * kvw, "none"), (br + 2 * kvw, 2 * kvw, "none"),
              (br + 4 * kvw, br, "silu")]
    qp, kdp, vdp, sgp = _inproj(xp, norm, w_p, plan_p, [BF16] * 4, TM_PROJ)
    keep = min(WINDOW, seq)
    w_kv = jnp.concatenate([wk, wv], axis=1).astype(BF16)
    x_tail = xp.reshape(batch, seq, d)[:, seq - keep:].reshape(batch * keep, d)
    kpf, vpf = _inproj(x_tail, norm, w_kv, [(0, kvw, "none"), (kvw, kvw, "none")], [F32, F32], TM_PROJ)
    w_s = w_in.astype(BF16)
    plan_s = [(0, br, "qscale"), (br, kvw, "none"), (br + kvw, kvw, "none"), (br + 2 * kvw, br, "silu")]
    qs, ksf, vsf, sgs = _inproj(xs, norm, w_s, plan_s, [BF16, F32, F32, BF16], TM_PROJ)

    tq = min(TQ_ATTN, seq)
    assert seq % tq == 0 and tq % WINDOW == 0
    band = (WIN_CHUNKS + 1) * CHUNK
    rel_p = (jnp.arange(band)[None, :] - WIN_CHUNKS * CHUNK) - jnp.arange(CHUNK)[:, None]
    bias_p = _bias_tiles(table, rel_p)
    og_p = _sw_prompt(qp, kdp, vdp, sgp, bias_p, sinks, batch, seq, tq, n_kv, group)

    k_all = jnp.concatenate([ck, ksf.reshape(dec_batch, t_dec, n_kv, hd)], axis=1)
    v_all = jnp.concatenate([cv, vsf.reshape(dec_batch, t_dec, n_kv, hd)], axis=1)
    kd_s = _dup_heads(k_all.reshape(dec_batch, wb + t_dec, kvw), n_kv, hd).astype(BF16)
    vd_s = _dup_heads(v_all.reshape(dec_batch, wb + t_dec, kvw), n_kv, hd).astype(BF16)
    rel_s = (past - wb + jnp.arange(wb + t_dec)[None, :]) - (past + jnp.arange(t_dec)[:, None])
    bias_s = _bias_tiles(table, rel_s)
    og_s = _sw_sample(qs, kd_s, vd_s, sgs, bias_s, sinks, dec_batch, t_dec, past, wb, n_kv, group)
    state = (kpf.reshape(batch, keep, n_kv, hd), vpf.reshape(batch, keep, n_kv, hd),
             k_all[:, -wb:], v_all[:, -wb:])
    return og_p, og_s, state


def kernel(x_prompt, x_sample, cache_k_0, cache_v_0, cache_k_1, cache_v_1, cache_k_2, cache_v_2,
           cache_k_3, cache_v_3, rel_bias_table,
           norm_0, w_in_0, w_out_0, da_lambda_0, da_subln_0,
           norm_1, w_in_1, w_out_1,
           norm_2, w_in_2, w_out_2, sw_sinks_2,
           norm_3, w_in_3, w_out_3, da_lambda_3, da_subln_3,
           final_norm):
    batch, seq, d = x_prompt.shape
    dec_batch, t_dec, _ = x_sample.shape
    dims = (batch, seq, dec_batch, t_dec, cache_k_0.shape[1], d)
    xp = x_prompt.reshape(batch * seq, d)
    xs = x_sample.reshape(dec_batch * t_dec, d)
    layers = [
        ("da", cache_k_0, cache_v_0, norm_0, w_in_0, w_out_0, (da_lambda_0, da_subln_0)),
        ("sb", cache_k_1, cache_v_1, norm_1, w_in_1, w_out_1, ()),
        ("sw", cache_k_2, cache_v_2, norm_2, w_in_2, w_out_2, (sw_sinks_2,)),
        ("da", cache_k_3, cache_v_3, norm_3, w_in_3, w_out_3, (da_lambda_3, da_subln_3)),
    ]
    states = []
    for i, (kind, ck, cv, norm, w_in, w_out, extra) in enumerate(layers):
        if kind == "da":
            lam_init = 0.8 - 0.6 * math.exp(-0.3 * i)
            og_p, og_s, st = _da_layer(xp, xs, ck, cv, rel_bias_table, norm, w_in, extra[0], extra[1],
                                       lam_init, dims)
        elif kind == "sb":
            og_p, og_s, st = _sb_layer(xp, xs, ck, cv, norm, w_in, dims)
        else:
            og_p, og_s, st = _sw_layer(xp, xs, ck, cv, rel_bias_table, norm, w_in, extra[0], dims)
        fg = final_norm if i == len(layers) - 1 else None
        w_o = w_out.astype(BF16)
        xp = _outproj(xp, og_p, w_o, fg, TM_PROJ)
        xs = _outproj(xs, og_s, w_o, fg, TM_PROJ)
        states.append(st)
    out = [xp.reshape(batch, seq, d), xs.reshape(dec_batch, t_dec, d)]
    for st in states:
        out.extend(st)
    return tuple(out)
```

```python
import functools
import math

import jax
import jax.numpy as jnp
import numpy as np
from jax import lax
from jax.experimental import pallas as pl
from jax.experimental.pallas import tpu as pltpu

F32 = jnp.float32
BF16 = jnp.bfloat16

EPS = 1e-6
NEG_INF = -1e30
LOG2E = math.log2(math.e)
CHUNK = 64
WINDOW = 128
WIN_CHUNKS = WINDOW // CHUNK
N_BUCKETS = 32
MAX_DISTANCE = 128
FAR_BUCKET = N_BUCKETS // 2 - 1
HEAD_SLAB = 128
HALF = HEAD_SLAB // 2
VMEM_LIMIT = 48 * 1024 * 1024

_NT = (((1,), (1,)), ((), ()))


def _cparams(sem):
    return pltpu.CompilerParams(dimension_semantics=sem, vmem_limit_bytes=VMEM_LIMIT)


def _inproj_kernel(*refs, plan, plan_t, has_nat):
    x_ref, g_ref = refs[:2]
    refs = refs[2:]
    x = x_ref[...]
    y = x * lax.rsqrt(jnp.mean(x * x, axis=-1, keepdims=True) + EPS)
    xn = (y * g_ref[...]).astype(BF16)
    if has_nat:
        w_ref, refs = refs[0], refs[1:]
    if plan_t:
        wt_ref, refs = refs[0], refs[1:]
    if has_nat:
        h = jnp.dot(xn, w_ref[...], preferred_element_type=F32)
        for o_ref, (c0, width, kind) in zip(refs, plan):
            t = h[:, c0:c0 + width]
            if kind == "qscale":
                t = t * (HALF ** -0.5)
            elif kind == "qscale2":
                t = t * (HALF ** -0.5 * LOG2E)
            elif kind == "silu":
                t = t * jax.nn.sigmoid(t)
            o_ref[...] = t.astype(o_ref.dtype)
        refs = refs[len(plan):]
    if plan_t:
        ht = lax.dot_general(wt_ref[...], xn, _NT, preferred_element_type=F32)
        for o_ref, (r0, nrows) in zip(refs, plan_t):
            o_ref[0] = ht[r0:r0 + nrows].astype(o_ref.dtype)


def _inproj(x2d, g, w_nat, plan, dtypes, tm, w_t=None, plan_t=(), dtypes_t=(), seq=None):
    t, d = x2d.shape
    tm = min(tm, t)
    assert t % tm == 0
    in_specs = [pl.BlockSpec((tm, d), lambda i: (i, 0)), pl.BlockSpec((1, d), lambda i: (0, 0))]
    args = [x2d, g.reshape(1, d)]
    out_shape, out_specs = [], []
    if w_nat is not None:
        in_specs.append(pl.BlockSpec(w_nat.shape, lambda i: (0, 0)))
        args.append(w_nat)
        out_shape += [jax.ShapeDtypeStruct((t, width), dt) for (_, width, _), dt in zip(plan, dtypes)]
        out_specs += [pl.BlockSpec((tm, width), lambda i: (i, 0)) for (_, width, _) in plan]
    if plan_t:
        assert seq % tm == 0 and t % seq == 0
        tps = seq // tm
        in_specs.append(pl.BlockSpec(w_t.shape, lambda i: (0, 0)))
        args.append(w_t)
        out_shape += [jax.ShapeDtypeStruct((t // seq, nrows, seq), dt) for (_, nrows), dt in zip(plan_t, dtypes_t)]
        out_specs += [pl.BlockSpec((1, nrows, tm), lambda i: (i // tps, 0, i % tps)) for (_, nrows) in plan_t]
    return pl.pallas_call(
        functools.partial(_inproj_kernel, plan=tuple(plan), plan_t=tuple(plan_t), has_nat=w_nat is not None),
        grid=(t // tm,),
        in_specs=in_specs,
        out_specs=out_specs,
        out_shape=out_shape,
        compiler_params=_cparams(("parallel",)),
        name="inproj",
    )(*args)


def _outproj_kernel(x_ref, og_ref, w_ref, *rest, final):
    xn = x_ref[...] + jnp.dot(og_ref[...], w_ref[...], preferred_element_type=F32)
    if final:
        g_ref, o_ref = rest
        y = xn * lax.rsqrt(jnp.mean(xn * xn, axis=-1, keepdims=True) + EPS)
        o_ref[...] = y * g_ref[...]
    else:
        (o_ref,) = rest
        o_ref[...] = xn


def _outproj(x2d, og, w_bf16, final_g, tm):
    t, d = x2d.shape
    br = og.shape[1]
    tm = min(tm, t)
    assert t % tm == 0
    in_specs = [
        pl.BlockSpec((tm, d), lambda i: (i, 0)),
        pl.BlockSpec((tm, br), lambda i: (i, 0)),
        pl.BlockSpec((br, d), lambda i: (0, 0)),
    ]
    args = [x2d, og, w_bf16]
    if final_g is not None:
        in_specs.append(pl.BlockSpec((1, d), lambda i: (0, 0)))
        args.append(final_g.reshape(1, d))
    return pl.pallas_call(
        functools.partial(_outproj_kernel, final=final_g is not None),
        grid=(t // tm,),
        in_specs=in_specs,
        out_specs=pl.BlockSpec((tm, d), lambda i: (i, 0)),
        out_shape=jax.ShapeDtypeStruct((t, d), F32),
        compiler_params=_cparams(("parallel",)),
        name="outproj",
    )(*args)


def _rel_bucket(rel):
    nb = N_BUCKETS // 2
    max_exact = nb // 2
    n = jnp.abs(rel)
    nf = jnp.maximum(n, 1).astype(F32)
    large = max_exact + (jnp.log(nf / max_exact) / math.log(MAX_DISTANCE / max_exact)
                         * (nb - max_exact)).astype(jnp.int32)
    large = jnp.minimum(large, nb - 1)
    return jnp.where(rel > 0, nb, 0) + jnp.where(n < max_exact, n, large)


def _bias_kernel(tab_ref, idx_ref, o_ref, *, scale):
    h = pl.program_id(0)
    idx = idx_ref[...]
    acc = jnp.zeros(idx.shape, F32)
    for b in range(N_BUCKETS):
        acc = jnp.where(idx == b, tab_ref[b, h], acc)
    o_ref[0] = acc * scale if scale != 1.0 else acc


def _bias_tiles(table, rel, scale=1.0):
    idx = _rel_bucket(rel).astype(jnp.int32)
    r, c = idx.shape
    nh = table.shape[1]
    return pl.pallas_call(
        functools.partial(_bias_kernel, scale=scale),
        grid=(nh,),
        in_specs=[
            pl.BlockSpec(memory_space=pltpu.SMEM),
            pl.BlockSpec((r, c), lambda h: (0, 0)),
        ],
        out_specs=pl.BlockSpec((1, r, c), lambda h: (h, 0, 0)),
        out_shape=jax.ShapeDtypeStruct((nh, r, c), F32),
        compiler_params=_cparams(("arbitrary",)),
        name="bias_tiles",
    )(table, idx)


def _split_halves(q_bf16):
    qf = q_bf16.astype(F32)
    lane = lax.broadcasted_iota(jnp.int32, qf.shape, 1)
    qa = jnp.where(lane < HALF, qf, 0.0).astype(BF16)
    qb = jnp.where(lane >= HALF, qf, 0.0).astype(BF16)
    return jnp.concatenate([qa, qb], axis=0)


def _join_halves(o, t):
    lane = lax.broadcasted_iota(jnp.int32, (t, HEAD_SLAB), 1)
    return jnp.where(lane < HALF, o[:t], o[t:])


def _softmax_first(s, v_bf16, m_scr, l_scr, acc_scr):
    m = jnp.max(s, axis=-1, keepdims=True)
    p = jnp.exp2(s - m)
    m_scr[...] = m
    l_scr[...] = jnp.sum(p, axis=-1, keepdims=True)
    acc_scr[...] = jnp.dot(p.astype(BF16), v_bf16, preferred_element_type=F32)


def _softmax_update(s, v_bf16, m_scr, l_scr, acc_scr):
    m = m_scr[...]
    m_new = jnp.maximum(m, jnp.max(s, axis=-1, keepdims=True))
    alpha = jnp.exp2(m - m_new)
    p = jnp.exp2(s - m_new)
    m_scr[...] = m_new
    l_scr[...] = alpha * l_scr[...] + jnp.sum(p, axis=-1, keepdims=True)
    acc_scr[...] = alpha * acc_scr[...] + jnp.dot(p.astype(BF16), v_bf16, preferred_element_type=F32)


def _da_finish(l, acc, t, lam, lam_init, subln, sg):
    o = acc[:t] / l[:t] - lam * (acc[t:] / l[t:])
    o = o * lax.rsqrt(jnp.mean(o * o, axis=-1, keepdims=True) + EPS) * subln
    o = o * (1.0 - lam_init)
    return (sg.astype(F32) * o).astype(BF16)


def _softmax_scratch(rows):
    return [pltpu.VMEM((rows, 1), F32), pltpu.VMEM((rows, 1), F32), pltpu.VMEM((rows, HEAD_SLAB), F32)]


def _lam_kernel(lp_ref, o_ref, *, lam_init):
    lp = lp_ref[...]
    a = jnp.sum(lp[0:1] * lp[1:2], axis=-1, keepdims=True)
    b = jnp.sum(lp[2:3] * lp[3:4], axis=-1, keepdims=True)
    o_ref[...] = jnp.exp(a) - jnp.exp(b) + lam_init


def _diff_lambda(lam_params, lam_init):
    return pl.pallas_call(
        functools.partial(_lam_kernel, lam_init=lam_init),
        out_shape=jax.ShapeDtypeStruct((1, 1), F32),
        name="diff_lambda",
    )(lam_params)


def _per_q_block(qi, nq, body):
    for qv in range(nq):
        pl.when(qi == qv)(functools.partial(body, qv))


def _da_prompt_kernel(tab_ref, lam_ref, q_ref, kt_ref, v_ref, sg_ref, bias_ref, subln_ref, o_ref,
                      *, tq, tkf, nq, lam_init, hps):
    hg = pl.program_id(1)
    rows = 2 * tq
    r = lax.broadcasted_iota(jnp.int32, (rows, tq), 0) % tq
    c = lax.broadcasted_iota(jnp.int32, (rows, tq), 1)
    row = lax.broadcasted_iota(jnp.int32, (rows, 1), 0)
    qqs, fars = [], []
    for hh in range(hps):
        h = hg * hps + hh
        qqs.append(_split_halves(q_ref[:, hh * HEAD_SLAB:(hh + 1) * HEAD_SLAB]))
        fars.append(LOG2E * jnp.where(row < tq, tab_ref[FAR_BUCKET, 2 * h], tab_ref[FAR_BUCKET, 2 * h + 1]))

    def body(qv):
        items = [(qv * tq, tq, "diag")]
        if qv >= 1:
            items.append(((qv - 1) * tq, tq, "prev"))
        nfar = max(qv - 1, 0) * tq
        for pos in range(0, nfar, tkf):
            items.append((pos, min(tkf, nfar - pos), "far"))
        work = [(hh, w) for w in range(len(items)) for hh in range(hps)]

        def scores(hh, w):
            st, n, kind = items[w]
            lanes = slice(hh * HEAD_SLAB, (hh + 1) * HEAD_SLAB)
            s = jnp.dot(qqs[hh], kt_ref[0, lanes, st:st + n], preferred_element_type=F32)
            if kind == "diag":
                s = s + bias_ref[2 * hh:2 * hh + 2, 0].reshape(rows, tq)
                s = jnp.where((c // CHUNK) <= (r // CHUNK), s, NEG_INF)
            elif kind == "prev":
                s = s + bias_ref[2 * hh:2 * hh + 2, 1].reshape(rows, tq)
            return s

        ahead = 2 * hps
        sc = {i: scores(*work[i]) for i in range(min(ahead, len(work)))}
        state = [dict(m=None, l=None, acc=None, far=False) for _ in range(hps)]
        for i, (hh, w) in enumerate(work):
            st, n, kind = items[w]
            z = state[hh]
            s = sc.pop(i)
            m = z["m"]
            if kind == "far" and not z["far"]:
                m = m - fars[hh]
                z["far"] = True
            bm = jnp.max(s, axis=-1, keepdims=True)
            if m is None:
                m_new = bm
                p = jnp.exp2(s - m_new)
                z["l"] = jnp.sum(p, axis=-1, keepdims=True)
            else:
                m_new = jnp.maximum(m, bm)
                alpha = jnp.exp2(m - m_new)
                p = jnp.exp2(s - m_new)
                z["l"] = alpha * z["l"] + jnp.sum(p, axis=-1, keepdims=True)
            if i + ahead < len(work):
                sc[i + ahead] = scores(*work[i + ahead])
            lanes = slice(hh * HEAD_SLAB, (hh + 1) * HEAD_SLAB)
            pv = jnp.dot(p.astype(BF16), v_ref[st:st + n, lanes], preferred_element_type=F32)
            z["acc"] = pv if m is None else alpha * z["acc"] + pv
            z["m"] = m_new
        for hh in range(hps):
            lanes = slice(hh * HEAD_SLAB, (hh + 1) * HEAD_SLAB)
            o_ref[:, lanes] = _da_finish(state[hh]["l"], state[hh]["acc"], tq, lam_ref[0, 0], lam_init,
                                         subln_ref[...], sg_ref[:, lanes])

    _per_q_block(pl.program_id(2), nq, body)


def _da_prompt(q, kt, v, sg, bias, table, lam, subln, batch, seq, tq, lam_init):
    t, width = q.shape
    hps = 2
    nh = width // HEAD_SLAB // hps
    nq = seq // tq
    w = hps * HEAD_SLAB
    return pl.pallas_call(
        functools.partial(_da_prompt_kernel, tq=tq, tkf=2 * tq, nq=nq, lam_init=lam_init, hps=hps),
        grid=(batch, nh, nq),
        in_specs=[
            pl.BlockSpec(memory_space=pltpu.SMEM),
            pl.BlockSpec(memory_space=pltpu.SMEM),
            pl.BlockSpec((tq, w), lambda b, h, i: (b * nq + i, h)),
            pl.BlockSpec((1, w, seq), lambda b, h, i: (b, h, 0)),
            pl.BlockSpec((seq, w), lambda b, h, i: (b, h)),
            pl.BlockSpec((tq, w), lambda b, h, i: (b * nq + i, h)),
            pl.BlockSpec((2 * hps, 2, tq, tq), lambda b, h, i: (h, 0, 0, 0)),
            pl.BlockSpec((1, HEAD_SLAB), lambda b, h, i: (0, 0)),
        ],
        out_specs=pl.BlockSpec((tq, w), lambda b, h, i: (b * nq + i, h)),
        out_shape=jax.ShapeDtypeStruct((t, width), BF16),
        compiler_params=_cparams(("parallel", "parallel", "arbitrary")),
        name="da_prompt",
    )(table, lam, q, kt, v, sg, bias, subln.reshape(1, HEAD_SLAB))


def _da_sample_kernel(lam_ref, q_ref, kct_ref, vc_ref, kn_ref, vn_ref, sg_ref, bc_ref, bn_ref,
                      subln_ref, o_ref, m_scr, l_scr, acc_scr, *, t, past, lam_init):
    qq = _split_halves(q_ref[...])
    state = (m_scr, l_scr, acc_scr)
    qpos = past + lax.broadcasted_iota(jnp.int32, (2 * t, 1), 0) % t
    kpos_n = past + lax.broadcasted_iota(jnp.int32, (1, t), 1)
    s = lax.dot_general(qq, kn_ref[...], _NT, preferred_element_type=F32) + bn_ref[...].reshape(2 * t, t)
    s = jnp.where((kpos_n // CHUNK) <= (qpos // CHUNK), s, NEG_INF)
    _softmax_first(s, vn_ref[...], *state)
    kpos_c = lax.broadcasted_iota(jnp.int32, (1, past), 1)
    s = jnp.dot(qq, kct_ref[0].astype(BF16), preferred_element_type=F32) + bc_ref[...].reshape(2 * t, past)
    s = jnp.where((kpos_c // CHUNK) <= (qpos // CHUNK), s, NEG_INF)
    _softmax_update(s, vc_ref[0].astype(BF16), *state)
    o_ref[...] = _da_finish(l_scr[...], acc_scr[...], t, lam_ref[0, 0], lam_init, subln_ref[...], sg_ref[...])


def _da_sample(q, ckt, cv, kn, vn, sg, bias_c, bias_n, lam, subln, batch, t, lam_init):
    tt, width = q.shape
    nh = width // HEAD_SLAB
    past = cv.shape[1]
    return pl.pallas_call(
        functools.partial(_da_sample_kernel, t=t, past=past, lam_init=lam_init),
        grid=(batch, nh),
        in_specs=[
            pl.BlockSpec(memory_space=pltpu.SMEM),
            pl.BlockSpec((t, HEAD_SLAB), lambda b, h: (b, h)),
            pl.BlockSpec((1, HEAD_SLAB, past), lambda b, h: (b, h, 0)),
            pl.BlockSpec((1, past, HEAD_SLAB), lambda b, h: (b, 0, h)),
            pl.BlockSpec((t, HEAD_SLAB), lambda b, h: (b, h)),
            pl.BlockSpec((t, HEAD_SLAB), lambda b, h: (b, h)),
            pl.BlockSpec((t, HEAD_SLAB), lambda b, h: (b, h)),
            pl.BlockSpec((2, t, past), lambda b, h: (h, 0, 0)),
            pl.BlockSpec((2, t, t), lambda b, h: (h, 0, 0)),
            pl.BlockSpec((1, HEAD_SLAB), lambda b, h: (0, 0)),
        ],
        out_specs=pl.BlockSpec((t, HEAD_SLAB), lambda b, h: (b, h)),
        out_shape=jax.ShapeDtypeStruct((tt, width), BF16),
        scratch_shapes=_softmax_scratch(2 * t),
        compiler_params=_cparams(("parallel", "parallel")),
        name="da_sample",
    )(lam, q, ckt, cv, kn, vn, sg, bias_c, bias_n, subln.reshape(1, HEAD_SLAB))


def _suffix_matrix(n):
    j = np.arange(n)[:, None]
    s = np.arange(n)[None, :]
    return jnp.asarray((j > s).astype(np.float32), dtype=BF16)


def _sb_logs(z, mask):
    lg = jnp.log(1.0 + jnp.exp2(-jnp.abs(z))) * LOG2E
    ls = jnp.minimum(z, 0.0) - lg
    lr = ls - z
    if mask is not None:
        lr = jnp.where(mask, lr, 0.0)
    return ls, lr


def _sb_suffix(lr, tmat):
    hi = lr.astype(BF16)
    lo = (lr - hi.astype(F32)).astype(BF16)
    return (jnp.dot(hi, tmat, preferred_element_type=F32)
            + jnp.dot(lo, tmat, preferred_element_type=F32))


def _sb_exp(ls, after, c, mask):
    arg = ls + after
    if c is not None:
        arg = arg + c
    a = jnp.exp2(arg)
    if mask is not None:
        a = jnp.where(mask, a, 0.0)
    return a.astype(BF16)


def _sb_weights(z, tmat, mask, c):
    ls, lr = _sb_logs(z, mask)
    a = _sb_exp(ls, _sb_suffix(lr, tmat), c, mask)
    return a, jnp.sum(lr, axis=-1, keepdims=True)


def _sb_prompt_kernel(q_ref, kt_ref, vt_ref, sg_ref, tmat_ref, o_ref, *, tq, nq):
    qq = _split_halves(q_ref[...])
    tmat = tmat_ref[...]
    r = lax.broadcasted_iota(jnp.int32, (2 * tq, tq), 0) % tq
    c = lax.broadcasted_iota(jnp.int32, (2 * tq, tq), 1)

    def body(qv):
        starts = [w * tq for w in range(qv, -1, -1)]
        nw = len(starts)

        def scores(w):
            return jnp.dot(qq, kt_ref[0, :, starts[w]:starts[w] + tq], preferred_element_type=F32)

        def finish(pending, acc):
            ls, after, rem, mask, w = pending
            a = _sb_exp(ls, after, rem, mask)
            pv = lax.dot_general(a, vt_ref[0, :, starts[w]:starts[w] + tq], _NT, preferred_element_type=F32)
            return pv if acc is None else acc + pv

        zs = {i: scores(i) for i in range(min(2, nw))}
        rem = acc = pending = None
        for w in range(nw):
            mask = (c < r) if w == 0 else None
            ls, lr = _sb_logs(zs.pop(w), mask)
            if w + 2 < nw:
                zs[w + 2] = scores(w + 2)
            if pending is not None:
                acc = finish(pending, acc)
            pending = (ls, _sb_suffix(lr, tmat), rem, mask, w)
            tot = jnp.sum(lr, axis=-1, keepdims=True)
            rem = tot if rem is None else rem + tot
        acc = finish(pending, acc)
        o_ref[...] = (sg_ref[...].astype(F32) * _join_halves(acc, tq)).astype(BF16)

    _per_q_block(pl.program_id(2), nq, body)


def _sb_prompt(q, kt, vt, sg, batch, seq, tq):
    t, width = q.shape
    nh = width // HEAD_SLAB
    nq = seq // tq
    return pl.pallas_call(
        functools.partial(_sb_prompt_kernel, tq=tq, nq=nq),
        grid=(batch, nh, nq),
        in_specs=[
            pl.BlockSpec((tq, HEAD_SLAB), lambda b, h, i: (b * nq + i, h)),
            pl.BlockSpec((1, HEAD_SLAB, seq), lambda b, h, i: (b, h, 0)),
            pl.BlockSpec((1, HEAD_SLAB, seq), lambda b, h, i: (b, h, 0)),
            pl.BlockSpec((tq, HEAD_SLAB), lambda b, h, i: (b * nq + i, h)),
            pl.BlockSpec((tq, tq), lambda b, h, i: (0, 0)),
        ],
        out_specs=pl.BlockSpec((tq, HEAD_SLAB), lambda b, h, i: (b * nq + i, h)),
        out_shape=jax.ShapeDtypeStruct((t, width), BF16),
        compiler_params=_cparams(("parallel", "parallel", "arbitrary")),
        name="sb_prompt",
    )(q, kt, vt, sg, _suffix_matrix(tq))


def _sb_sample_kernel(q_ref, kct_ref, vct_ref, kn_ref, vn_ref, sg_ref, tmat_ref, o_ref, *, t, tn, tk, past):
    qq = _split_halves(q_ref[...])
    tmat = tmat_ref[...]
    pad = jnp.zeros((tn - t, HEAD_SLAB), BF16)
    kn = jnp.concatenate([kn_ref[...], pad], axis=0)
    vn = jnp.concatenate([vn_ref[...], pad], axis=0)
    r = lax.broadcasted_iota(jnp.int32, (2 * t, tn), 0) % t
    c = lax.broadcasted_iota(jnp.int32, (2 * t, tn), 1)
    z = lax.dot_general(qq, kn, _NT, preferred_element_type=F32)
    a, rem = _sb_weights(z, tmat[:tn, :tn], c < r, None)
    acc = jnp.dot(a, vn, preferred_element_type=F32)
    for j in reversed(range(past // tk)):
        kj = kct_ref[0, :, j * tk:(j + 1) * tk].astype(BF16)
        vj = vct_ref[0, :, j * tk:(j + 1) * tk].astype(BF16)
        z = jnp.dot(qq, kj, preferred_element_type=F32)
        a, tot = _sb_weights(z, tmat, None, rem)
        acc = acc + lax.dot_general(a, vj, _NT, preferred_element_type=F32)
        rem = rem + tot
    o = _join_halves(acc, t)
    o_ref[...] = (sg_ref[...].astype(F32) * o).astype(BF16)


def _sb_sample(q, ckt, cvt, kn, vn, sg, batch, t, tk):
    tt, width = q.shape
    nh = width // HEAD_SLAB
    past = ckt.shape[2]
    tk = min(tk, past)
    assert past % tk == 0
    tn = HEAD_SLAB
    assert t <= tn <= tk
    return pl.pallas_call(
        functools.partial(_sb_sample_kernel, t=t, tn=tn, tk=tk, past=past),
        grid=(batch, nh),
        in_specs=[
            pl.BlockSpec((t, HEAD_SLAB), lambda b, h: (b, h)),
            pl.BlockSpec((1, HEAD_SLAB, past), lambda b, h: (b, h, 0)),
            pl.BlockSpec((1, HEAD_SLAB, past), lambda b, h: (b, h, 0)),
            pl.BlockSpec((t, HEAD_SLAB), lambda b, h: (b, h)),
            pl.BlockSpec((t, HEAD_SLAB), lambda b, h: (b, h)),
            pl.BlockSpec((t, HEAD_SLAB), lambda b, h: (b, h)),
            pl.BlockSpec((tk, tk), lambda b, h: (0, 0)),
        ],
        out_specs=pl.BlockSpec((t, HEAD_SLAB), lambda b, h: (b, h)),
        out_shape=jax.ShapeDtypeStruct((tt, width), BF16),
        compiler_params=_cparams(("parallel", "parallel")),
        name="sb_sample",
    )(q, ckt, cvt, kn, vn, sg, _suffix_matrix(tk))


def _sw_group(sink_ref, q_slabs, kk, vv, bias, valid, kvh, group, t):
    qst = jnp.concatenate([_split_halves(qs) for qs in q_slabs], axis=0)
    s = lax.dot_general(qst, kk, _NT, preferred_element_type=F32) + bias
    s = jnp.where(valid, s, NEG_INF)
    row = lax.broadcasted_iota(jnp.int32, (group * t, 1), 0)
    sk = jnp.zeros((group * t, 1), F32)
    for g in range(group):
        sk = jnp.where(row // t == g, sink_ref[kvh * group + g], sk)
    m = jnp.maximum(jnp.max(s, axis=-1, keepdims=True), sk)
    p = jnp.exp(s - m)
    w = p / (jnp.sum(p, axis=-1, keepdims=True) + jnp.exp(sk - m))
    o = jnp.dot(w.astype(BF16), vv, preferred_element_type=F32)
    return [_join_halves(o[2 * i * t:(2 * i + 2) * t], t) for i in range(group // 2)]


def _sw_prompt_kernel(sink_ref, q_ref, kp_ref, kc_ref, vp_ref, vc_ref, sg_ref, bias_ref, o_ref,
                      *, tq, n_kv, group):
    qi = pl.program_id(1)
    band = (WIN_CHUNKS + 1) * CHUNK
    kcat = jnp.concatenate([kp_ref[...], kc_ref[...]], axis=0)
    vcat = jnp.concatenate([vp_ref[...], vc_ref[...]], axis=0)
    col = lax.broadcasted_iota(jnp.int32, (1, band), 1)
    for cc in range(tq // CHUNK):
        r0 = cc * CHUNK
        valid = (col >= WINDOW - r0) | (qi > 0)
        for kvh in range(n_kv):
            kk = kcat[r0:r0 + band, kvh * HEAD_SLAB:(kvh + 1) * HEAD_SLAB]
            vv = vcat[r0:r0 + band, kvh * HEAD_SLAB:(kvh + 1) * HEAD_SLAB]
            s0 = kvh * (group // 2)
            q_slabs = [q_ref[r0:r0 + CHUNK, (s0 + i) * HEAD_SLAB:(s0 + i + 1) * HEAD_SLAB]
                       for i in range(group // 2)]
            bias = bias_ref[kvh * group:(kvh + 1) * group].reshape(group * CHUNK, band)
            outs = _sw_group(sink_ref, q_slabs, kk, vv, bias, valid, kvh, group, CHUNK)
            for i, o in enumerate(outs):
                lanes = slice((s0 + i) * HEAD_SLAB, (s0 + i + 1) * HEAD_SLAB)
                sg = sg_ref[r0:r0 + CHUNK, lanes].astype(F32)
                o_ref[r0:r0 + CHUNK, lanes] = (sg * o).astype(BF16)


def _sw_prompt(q, kd, vd, sg, bias, sinks, batch, seq, tq, n_kv, group):
    t, width = q.shape
    nq = seq // tq
    wpb = tq // WINDOW
    kvw = kd.shape[1]
    band = (WIN_CHUNKS + 1) * CHUNK
    prev = lambda b, i: (jnp.maximum((b * nq + i) * wpb - 1, 0), 0)
    cur = lambda b, i: (b * nq + i, 0)
    return pl.pallas_call(
        functools.partial(_sw_prompt_kernel, tq=tq, n_kv=n_kv, group=group),
        grid=(batch, nq),
        in_specs=[
            pl.BlockSpec(memory_space=pltpu.SMEM),
            pl.BlockSpec((tq, width), cur),
            pl.BlockSpec((WINDOW, kvw), prev),
            pl.BlockSpec((tq, kvw), cur),
            pl.BlockSpec((WINDOW, kvw), prev),
            pl.BlockSpec((tq, kvw), cur),
            pl.BlockSpec((tq, width), cur),
            pl.BlockSpec((n_kv * group, CHUNK, band), lambda b, i: (0, 0, 0)),
        ],
        out_specs=pl.BlockSpec((tq, width), cur),
        out_shape=jax.ShapeDtypeStruct((t, width), BF16),
        compiler_params=_cparams(("parallel", "arbitrary")),
        name="sw_prompt",
    )(sinks, q, kd, kd, vd, vd, sg, bias)


def _sw_sample_kernel(sink_ref, q_ref, k_ref, v_ref, sg_ref, bias_ref, o_ref, *, t, past, wb, n_kv, group):
    band = wb + t
    qpos = past + lax.broadcasted_iota(jnp.int32, (group * t, 1), 0) % t
    kpos = past - wb + lax.broadcasted_iota(jnp.int32, (1, band), 1)
    qc = qpos // CHUNK
    kc = kpos // CHUNK
    valid = (kc <= qc) & (kc >= qc - WIN_CHUNKS)
    for kvh in range(n_kv):
        kk = k_ref[0, :, kvh * HEAD_SLAB:(kvh + 1) * HEAD_SLAB]
        vv = v_ref[0, :, kvh * HEAD_SLAB:(kvh + 1) * HEAD_SLAB]
        s0 = kvh * (group // 2)
        q_slabs = [q_ref[:, (s0 + i) * HEAD_SLAB:(s0 + i + 1) * HEAD_SLAB] for i in range(group // 2)]
        bias = bias_ref[kvh * group:(kvh + 1) * group].reshape(group * t, band)
        outs = _sw_group(sink_ref, q_slabs, kk, vv, bias, valid, kvh, group, t)
        for i, o in enumerate(outs):
            lanes = slice((s0 + i) * HEAD_SLAB, (s0 + i + 1) * HEAD_SLAB)
            o_ref[:, lanes] = (sg_ref[:, lanes].astype(F32) * o).astype(BF16)


def _sw_sample(q, kd, vd, sg, bias, sinks, batch, t, past, wb, n_kv, group):
    tt, width = q.shape
    band, kvw = kd.shape[1:]
    return pl.pallas_call(
        functools.partial(_sw_sample_kernel, t=t, past=past, wb=wb, n_kv=n_kv, group=group),
        grid=(batch,),
        in_specs=[
            pl.BlockSpec(memory_space=pltpu.SMEM),
            pl.BlockSpec((t, width), lambda b: (b, 0)),
            pl.BlockSpec((1, band, kvw), lambda b: (b, 0, 0)),
            pl.BlockSpec((1, band, kvw), lambda b: (b, 0, 0)),
            pl.BlockSpec((t, width), lambda b: (b, 0)),
            pl.BlockSpec((n_kv * group, t, band), lambda b: (0, 0, 0)),
        ],
        out_specs=pl.BlockSpec((t, width), lambda b: (b, 0)),
        out_shape=jax.ShapeDtypeStruct((tt, width), BF16),
        compiler_params=_cparams(("parallel",)),
        name="sw_sample",
    )(sinks, q, kd, vd, sg, bias)


TM_PROJ = 256
TQ_ATTN = 256


def _feature_major(a):
    nd = a.ndim
    a = jnp.transpose(a, (0,) + tuple(range(2, nd)) + (1,))
    return a.reshape(a.shape[0], -1, a.shape[-1])


def _position_major(a, feat_shape):
    b, _, p = a.shape
    a = a.reshape((b,) + tuple(feat_shape) + (p,))
    nd = a.ndim
    return jnp.transpose(a, (0, nd - 1) + tuple(range(1, nd - 1)))


def _da_layer(xp, xs, ck, cv, table, norm, w_in, lam_params, subln, lam_init, dims):
    batch, seq, dec_batch, t_dec, past, d = dims
    br = d
    wq, wk, wv, wg = (w_in[:, i * br:(i + 1) * br] for i in range(4))
    w_p = jnp.concatenate([wq, wv, wg], axis=1).astype(BF16)
    plan_p = [(0, br, "qscale2"), (br, br, "none"), (2 * br, br, "silu"), (br, br, "none")]
    qp, vp, sgp, vpf, ktf, kt = _inproj(
        xp, norm, w_p, plan_p, [BF16, BF16, BF16, F32], TM_PROJ,
        w_t=wk.T.astype(BF16), plan_t=[(0, br), (0, br)], dtypes_t=[F32, BF16], seq=seq)
    plan_s = [(0, br, "qscale2"), (br, br, "none"), (2 * br, br, "none"), (3 * br, br, "silu"),
              (br, br, "none"), (2 * br, br, "none")]
    qs, ks, vs, sgs, ksf, vsf = _inproj(xs, norm, w_in.astype(BF16), plan_s,
                                        [BF16, BF16, BF16, BF16, F32, F32], TM_PROJ)
    lam = _diff_lambda(lam_params, lam_init)
    tq = min(TQ_ATTN, seq)
    assert seq % tq == 0 and tq >= MAX_DISTANCE and tq % CHUNK == 0
    assert past == ck.shape[1] and past % CHUNK == 0 and t_dec <= CHUNK
    i = jnp.arange(tq)[:, None]
    j = jnp.arange(tq)[None, :]
    rel_p = jnp.concatenate([j - i, j - tq - i], axis=0)
    nbh = table.shape[1]
    bias_p = _bias_tiles(table, rel_p, LOG2E).reshape(nbh, 2, tq, tq)
    og_p = _da_prompt(qp, kt, vp, sgp, bias_p, table, lam, subln, batch, seq, tq, lam_init)
    q_pos = past + jnp.arange(t_dec)[:, None]
    bias_c = _bias_tiles(table, jnp.arange(past)[None, :] - q_pos, LOG2E)
    bias_n = _bias_tiles(table, past + jnp.arange(t_dec)[None, :] - q_pos, LOG2E)
    og_s = _da_sample(qs, _feature_major(ck), cv.reshape(dec_batch, past, br), ks, vs, sgs,
                      bias_c, bias_n, lam, subln, dec_batch, t_dec, lam_init)
    state = (_position_major(ktf, ck.shape[2:]), vpf.reshape((batch, seq) + cv.shape[2:]),
             ksf.reshape((dec_batch, t_dec) + ck.shape[2:]), vsf.reshape((dec_batch, t_dec) + cv.shape[2:]))
    return og_p, og_s, state


def _sb_layer(xp, xs, ck, cv, norm, w_in, dims):
    batch, seq, dec_batch, t_dec, past, d = dims
    br = d
    wq, wk, wv, wg = (w_in[:, i * br:(i + 1) * br] for i in range(4))
    w_p = jnp.concatenate([wq, wg], axis=1).astype(BF16)
    w_t = jnp.concatenate([wk, wv], axis=1).T.astype(BF16)
    qp, sgp, ktf, kt, vtf, vt = _inproj(
        xp, norm, w_p, [(0, br, "qscale2"), (br, br, "silu")], [BF16, BF16], TM_PROJ,
        w_t=w_t, plan_t=[(0, br), (0, br), (br, br), (br, br)], dtypes_t=[F32, BF16, F32, BF16], seq=seq)
    plan_s = [(0, br, "qscale2"), (br, br, "none"), (2 * br, br, "none"), (3 * br, br, "silu"),
              (br, br, "none"), (2 * br, br, "none")]
    qs, ks, vs, sgs, ksf, vsf = _inproj(xs, norm, w_in.astype(BF16), plan_s,
                                        [BF16, BF16, BF16, BF16, F32, F32], TM_PROJ)
    tq = min(TQ_ATTN, seq)
    assert seq % tq == 0 and past == ck.shape[1]
    og_p = _sb_prompt(qp, kt, vt, sgp, batch, seq, tq)
    og_s = _sb_sample(qs, _feature_major(ck), _feature_major(cv), ks, vs, sgs, dec_batch, t_dec, TQ_ATTN)
    state = (_position_major(ktf, ck.shape[2:]), _position_major(vtf, cv.shape[2:]),
             ksf.reshape((dec_batch, t_dec) + ck.shape[2:]), vsf.reshape((dec_batch, t_dec) + cv.shape[2:]))
    return og_p, og_s, state


def _dup_heads(a, n_kv, hd):
    lead = a.shape[:-1]
    a = a.reshape(lead + (n_kv, 1, hd))
    return jnp.broadcast_to(a, lead + (n_kv, 2, hd)).reshape(lead + (n_kv * 2 * hd,))


def _sw_layer(xp, xs, ck, cv, table, norm, w_in, sinks, dims):
    batch, seq, dec_batch, t_dec, past, d = dims
    br = d
    wb, n_kv, hd = ck.shape[1:]
    assert hd == HALF
    kvw = n_kv * hd
    group = br // hd // n_kv
    assert group % 2 == 0
    wq, wk, wv, wg = (w_in[:, :br], w_in[:, br:br + kvw], w_in[:, br + kvw:br + 2 * kvw],
                      w_in[:, br + 2 * kvw:])
    w_p = jnp.concatenate([wq, _dup_heads(wk, n_kv, hd), _dup_heads(wv, n_kv, hd), wg], axis=1).astype(BF16)
    plan_p = [(0, br, "qscale"), (br, 2 * kvw, "none"), (br + 2 * kvw, 2 * kvw, "none"),
              (br + 4 * kvw, br, "silu")]
    qp, kdp, vdp, sgp = _inproj(xp, norm, w_p, plan_p, [BF16] * 4, TM_PROJ)
    keep = min(WINDOW, seq)
    w_kv_t = jnp.concatenate([wk, wv], axis=1).T.astype(BF16)
    x_tail = xp.reshape(batch, seq, d)[:, seq - keep:].reshape(batch * keep, d)
    ktf, vtf = _inproj(x_tail, norm, None, [], [], keep, w_t=w_kv_t,
                       plan_t=[(0, kvw), (kvw, kvw)], dtypes_t=[F32, F32], seq=keep)
    plan_s = [(0, br, "qscale"), (br, kvw, "none"), (br + kvw, kvw, "none"), (br + 2 * kvw, br, "silu")]
    qs, ksf, vsf, sgs = _inproj(xs, norm, w_in.astype(BF16), plan_s, [BF16, F32, F32, BF16], TM_PROJ)

    tq = min(TQ_ATTN, seq)
    assert seq % tq == 0 and tq % WINDOW == 0
    band = (WIN_CHUNKS + 1) * CHUNK
    rel_p = (jnp.arange(band)[None, :] - WIN_CHUNKS * CHUNK) - jnp.arange(CHUNK)[:, None]
    bias_p = _bias_tiles(table, rel_p)
    og_p = _sw_prompt(qp, kdp, vdp, sgp, bias_p, sinks, batch, seq, tq, n_kv, group)

    k_all = jnp.concatenate([ck, ksf.reshape(dec_batch, t_dec, n_kv, hd)], axis=1)
    v_all = jnp.concatenate([cv, vsf.reshape(dec_batch, t_dec, n_kv, hd)], axis=1)
    kd_s = _dup_heads(k_all.reshape(dec_batch, wb + t_dec, kvw), n_kv, hd).astype(BF16)
    vd_s = _dup_heads(v_all.reshape(dec_batch, wb + t_dec, kvw), n_kv, hd).astype(BF16)
    rel_s = (past - wb + jnp.arange(wb + t_dec)[None, :]) - (past + jnp.arange(t_dec)[:, None])
    bias_s = _bias_tiles(table, rel_s)
    og_s = _sw_sample(qs, kd_s, vd_s, sgs, bias_s, sinks, dec_batch, t_dec, past, wb, n_kv, group)
    state = (_position_major(ktf, (n_kv, hd)), _position_major(vtf, (n_kv, hd)),
             k_all[:, -wb:], v_all[:, -wb:])
    return og_p, og_s, state


def kernel(x_prompt, x_sample, cache_k_0, cache_v_0, cache_k_1, cache_v_1, cache_k_2, cache_v_2,
           cache_k_3, cache_v_3, rel_bias_table,
           norm_0, w_in_0, w_out_0, da_lambda_0, da_subln_0,
           norm_1, w_in_1, w_out_1,
           norm_2, w_in_2, w_out_2, sw_sinks_2,
           norm_3, w_in_3, w_out_3, da_lambda_3, da_subln_3,
           final_norm):
    batch, seq, d = x_prompt.shape
    dec_batch, t_dec, _ = x_sample.shape
    dims = (batch, seq, dec_batch, t_dec, cache_k_0.shape[1], d)
    xp = x_prompt.reshape(batch * seq, d)
    xs = x_sample.reshape(dec_batch * t_dec, d)
    layers = [
        ("da", cache_k_0, cache_v_0, norm_0, w_in_0, w_out_0, (da_lambda_0, da_subln_0)),
        ("sb", cache_k_1, cache_v_1, norm_1, w_in_1, w_out_1, ()),
        ("sw", cache_k_2, cache_v_2, norm_2, w_in_2, w_out_2, (sw_sinks_2,)),
        ("da", cache_k_3, cache_v_3, norm_3, w_in_3, w_out_3, (da_lambda_3, da_subln_3)),
    ]
    states = []
    for i, (kind, ck, cv, norm, w_in, w_out, extra) in enumerate(layers):
        if kind == "da":
            lam_init = 0.8 - 0.6 * math.exp(-0.3 * i)
            og_p, og_s, st = _da_layer(xp, xs, ck, cv, rel_bias_table, norm, w_in, extra[0], extra[1],
                                       lam_init, dims)
        elif kind == "sb":
            og_p, og_s, st = _sb_layer(xp, xs, ck, cv, norm, w_in, dims)
        else:
            og_p, og_s, st = _sw_layer(xp, xs, ck, cv, rel_bias_table, norm, w_in, extra[0], dims)
        fg = final_norm if i == len(layers) - 1 else None
        w_o = w_out.astype(BF16)
        xp = _outproj(xp, og_p, w_o, fg, TM_PROJ)
        xs = _outproj(xs, og_s, w_o, fg, TM_PROJ)
        states.append(st)
    out = [xp.reshape(batch, seq, d), xs.reshape(dec_batch, t_dec, d)]
    for st in states:
        out.extend(st)
    return tuple(out)
```

```python
import functools
import math

import jax
import jax.numpy as jnp
import numpy as np
from jax import lax
from jax.experimental import pallas as pl
from jax.experimental.pallas import tpu as pltpu

F32 = jnp.float32
BF16 = jnp.bfloat16

EPS = 1e-6
NEG_INF = -1e30
LOG2E = math.log2(math.e)
CHUNK = 64
WINDOW = 128
WIN_CHUNKS = WINDOW // CHUNK
N_BUCKETS = 32
MAX_DISTANCE = 128
FAR_BUCKET = N_BUCKETS // 2 - 1
HEAD_SLAB = 128
HALF = HEAD_SLAB // 2
VMEM_LIMIT = 48 * 1024 * 1024

_NT = (((1,), (1,)), ((), ()))


def _cparams(sem):
    return pltpu.CompilerParams(dimension_semantics=sem, vmem_limit_bytes=VMEM_LIMIT)


def _inproj_kernel(*refs, plan, plan_t, has_nat, has_res):
    x_ref, refs = refs[0], refs[1:]
    x = x_ref[...]
    if has_res:
        (og_ref, wo_ref), refs = refs[:2], refs[2:]
        x = x + jnp.dot(og_ref[...], wo_ref[...], preferred_element_type=F32)
    g_ref, refs = refs[0], refs[1:]
    if has_nat:
        w_ref, refs = refs[0], refs[1:]
    if plan_t:
        wt_ref, refs = refs[0], refs[1:]
    if has_res:
        refs[0][...] = x
        refs = refs[1:]
    y = x * lax.rsqrt(jnp.mean(x * x, axis=-1, keepdims=True) + EPS)
    xn = (y * g_ref[...]).astype(BF16)
    if has_nat:
        h = jnp.dot(xn, w_ref[...], preferred_element_type=F32)
        for o_ref, (c0, width, kind) in zip(refs, plan):
            t = h[:, c0:c0 + width]
            if kind == "qscale":
                t = t * (HALF ** -0.5)
            elif kind == "qscale2":
                t = t * (HALF ** -0.5 * LOG2E)
            elif kind == "silu":
                t = t * jax.nn.sigmoid(t)
            o_ref[...] = t.astype(o_ref.dtype)
        refs = refs[len(plan):]
    if plan_t:
        ht = lax.dot_general(wt_ref[...], xn, _NT, preferred_element_type=F32)
        for o_ref, (r0, nrows) in zip(refs, plan_t):
            o_ref[0] = ht[r0:r0 + nrows].astype(o_ref.dtype)


def _inproj(x2d, g, w_nat, plan, dtypes, tm, w_t=None, plan_t=(), dtypes_t=(), seq=None, res=None):
    t, d = x2d.shape
    tm = min(tm, t)
    assert t % tm == 0
    in_specs = [pl.BlockSpec((tm, d), lambda i: (i, 0))]
    args = [x2d]
    out_shape, out_specs = [], []
    if res is not None:
        og, w_out = res
        in_specs += [pl.BlockSpec((tm, og.shape[1]), lambda i: (i, 0)), pl.BlockSpec(w_out.shape, lambda i: (0, 0))]
        args += [og, w_out]
        out_shape.append(jax.ShapeDtypeStruct((t, d), F32))
        out_specs.append(pl.BlockSpec((tm, d), lambda i: (i, 0)))
    in_specs.append(pl.BlockSpec((1, d), lambda i: (0, 0)))
    args.append(g.reshape(1, d))
    if w_nat is not None:
        in_specs.append(pl.BlockSpec(w_nat.shape, lambda i: (0, 0)))
        args.append(w_nat)
        out_shape += [jax.ShapeDtypeStruct((t, width), dt) for (_, width, _), dt in zip(plan, dtypes)]
        out_specs += [pl.BlockSpec((tm, width), lambda i: (i, 0)) for (_, width, _) in plan]
    if plan_t:
        assert seq % tm == 0 and t % seq == 0
        tps = seq // tm
        in_specs.append(pl.BlockSpec(w_t.shape, lambda i: (0, 0)))
        args.append(w_t)
        out_shape += [jax.ShapeDtypeStruct((t // seq, nrows, seq), dt) for (_, nrows), dt in zip(plan_t, dtypes_t)]
        out_specs += [pl.BlockSpec((1, nrows, tm), lambda i: (i // tps, 0, i % tps)) for (_, nrows) in plan_t]
    return pl.pallas_call(
        functools.partial(_inproj_kernel, plan=tuple(plan), plan_t=tuple(plan_t), has_nat=w_nat is not None,
                          has_res=res is not None),
        grid=(t // tm,),
        in_specs=in_specs,
        out_specs=out_specs,
        out_shape=out_shape,
        compiler_params=_cparams(("parallel",)),
        name="inproj",
    )(*args)


def _outproj_kernel(x_ref, og_ref, w_ref, g_ref, o_ref):
    x = x_ref[...] + jnp.dot(og_ref[...], w_ref[...], preferred_element_type=F32)
    y = x * lax.rsqrt(jnp.mean(x * x, axis=-1, keepdims=True) + EPS)
    o_ref[...] = y * g_ref[...]


def _outproj(x2d, og, w_bf16, final_g, tm):
    t, d = x2d.shape
    br = og.shape[1]
    tm = min(tm, t)
    assert t % tm == 0
    return pl.pallas_call(
        _outproj_kernel,
        grid=(t // tm,),
        in_specs=[
            pl.BlockSpec((tm, d), lambda i: (i, 0)),
            pl.BlockSpec((tm, br), lambda i: (i, 0)),
            pl.BlockSpec((br, d), lambda i: (0, 0)),
            pl.BlockSpec((1, d), lambda i: (0, 0)),
        ],
        out_specs=pl.BlockSpec((tm, d), lambda i: (i, 0)),
        out_shape=jax.ShapeDtypeStruct((t, d), F32),
        compiler_params=_cparams(("parallel",)),
        name="outproj",
    )(x2d, og, w_bf16, final_g.reshape(1, d))


def _rel_bucket(rel):
    nb = N_BUCKETS // 2
    max_exact = nb // 2
    n = jnp.abs(rel)
    nf = jnp.maximum(n, 1).astype(F32)
    large = max_exact + (jnp.log(nf / max_exact) / math.log(MAX_DISTANCE / max_exact)
                         * (nb - max_exact)).astype(jnp.int32)
    large = jnp.minimum(large, nb - 1)
    return jnp.where(rel > 0, nb, 0) + jnp.where(n < max_exact, n, large)


def _bias_kernel(tab_ref, idx_ref, o_ref, *, scale):
    h = pl.program_id(0)
    idx = idx_ref[...]
    acc = jnp.zeros(idx.shape, F32)
    for b in range(N_BUCKETS):
        acc = jnp.where(idx == b, tab_ref[b, h], acc)
    o_ref[0] = acc * scale if scale != 1.0 else acc


def _bias_tiles(table, rel, scale=1.0):
    idx = _rel_bucket(rel).astype(jnp.int32)
    r, c = idx.shape
    nh = table.shape[1]
    return pl.pallas_call(
        functools.partial(_bias_kernel, scale=scale),
        grid=(nh,),
        in_specs=[
            pl.BlockSpec(memory_space=pltpu.SMEM),
            pl.BlockSpec((r, c), lambda h: (0, 0)),
        ],
        out_specs=pl.BlockSpec((1, r, c), lambda h: (h, 0, 0)),
        out_shape=jax.ShapeDtypeStruct((nh, r, c), F32),
        compiler_params=_cparams(("arbitrary",)),
        name="bias_tiles",
    )(table, idx)


def _split_halves(q_bf16):
    qf = q_bf16.astype(F32)
    lane = lax.broadcasted_iota(jnp.int32, qf.shape, 1)
    qa = jnp.where(lane < HALF, qf, 0.0).astype(BF16)
    qb = jnp.where(lane >= HALF, qf, 0.0).astype(BF16)
    return jnp.concatenate([qa, qb], axis=0)


def _join_halves(o, t):
    lane = lax.broadcasted_iota(jnp.int32, (t, HEAD_SLAB), 1)
    return jnp.where(lane < HALF, o[:t], o[t:])


def _da_finish(l, acc, t, lam, lam_init, subln, sg):
    o = acc[:t] / l[:t] - lam * (acc[t:] / l[t:])
    o = o * lax.rsqrt(jnp.mean(o * o, axis=-1, keepdims=True) + EPS) * subln
    o = o * (1.0 - lam_init)
    return (sg.astype(F32) * o).astype(BF16)


def _lam_kernel(lp_ref, o_ref, *, lam_init):
    lp = lp_ref[...]
    a = jnp.sum(lp[0:1] * lp[1:2], axis=-1, keepdims=True)
    b = jnp.sum(lp[2:3] * lp[3:4], axis=-1, keepdims=True)
    o_ref[...] = jnp.exp(a) - jnp.exp(b) + lam_init


def _diff_lambda(lam_params, lam_init):
    return pl.pallas_call(
        functools.partial(_lam_kernel, lam_init=lam_init),
        out_shape=jax.ShapeDtypeStruct((1, 1), F32),
        name="diff_lambda",
    )(lam_params)


def _per_q_block(qi, nq, body):
    for qv in range(nq):
        pl.when(qi == qv)(functools.partial(body, qv))


def _da_prompt_kernel(tab_ref, lam_ref, q_ref, kt_ref, v_ref, sg_ref, bias_ref, subln_ref, o_ref,
                      *, tq, tkf, nq, lam_init, hps):
    hg = pl.program_id(1)
    rows = 2 * tq
    r = lax.broadcasted_iota(jnp.int32, (rows, tq), 0) % tq
    c = lax.broadcasted_iota(jnp.int32, (rows, tq), 1)
    row = lax.broadcasted_iota(jnp.int32, (rows, 1), 0)
    qqs, fars = [], []
    for hh in range(hps):
        h = hg * hps + hh
        qqs.append(_split_halves(q_ref[:, hh * HEAD_SLAB:(hh + 1) * HEAD_SLAB]))
        fars.append(LOG2E * jnp.where(row < tq, tab_ref[FAR_BUCKET, 2 * h], tab_ref[FAR_BUCKET, 2 * h + 1]))

    def body(qv):
        items = [(qv * tq, tq, "diag")]
        if qv >= 1:
            items.append(((qv - 1) * tq, tq, "prev"))
        nfar = max(qv - 1, 0) * tq
        for pos in range(0, nfar, tkf):
            items.append((pos, min(tkf, nfar - pos), "far"))
        work = [(hh, w) for w in range(len(items)) for hh in range(hps)]

        def scores(hh, w):
            st, n, kind = items[w]
            lanes = slice(hh * HEAD_SLAB, (hh + 1) * HEAD_SLAB)
            s = jnp.dot(qqs[hh], kt_ref[0, lanes, st:st + n], preferred_element_type=F32)
            if kind == "diag":
                s = s + bias_ref[2 * hh:2 * hh + 2, 0].reshape(rows, tq)
                s = jnp.where((c // CHUNK) <= (r // CHUNK), s, NEG_INF)
            elif kind == "prev":
                s = s + bias_ref[2 * hh:2 * hh + 2, 1].reshape(rows, tq)
            return s

        ahead = 2 * hps
        sc = {i: scores(*work[i]) for i in range(min(ahead, len(work)))}
        state = [dict(m=None, l=None, acc=None, far=False) for _ in range(hps)]
        for i, (hh, w) in enumerate(work):
            st, n, kind = items[w]
            z = state[hh]
            s = sc.pop(i)
            m = z["m"]
            if kind == "far" and not z["far"]:
                m = m - fars[hh]
                z["far"] = True
            bm = jnp.max(s, axis=-1, keepdims=True)
            if m is None:
                m_new = bm
                p = jnp.exp2(s - m_new)
                z["l"] = jnp.sum(p, axis=-1, keepdims=True)
            else:
                m_new = jnp.maximum(m, bm)
                alpha = jnp.exp2(m - m_new)
                p = jnp.exp2(s - m_new)
                z["l"] = alpha * z["l"] + jnp.sum(p, axis=-1, keepdims=True)
            if i + ahead < len(work):
                sc[i + ahead] = scores(*work[i + ahead])
            lanes = slice(hh * HEAD_SLAB, (hh + 1) * HEAD_SLAB)
            pv = jnp.dot(p.astype(BF16), v_ref[st:st + n, lanes], preferred_element_type=F32)
            z["acc"] = pv if m is None else alpha * z["acc"] + pv
            z["m"] = m_new
        for hh in range(hps):
            lanes = slice(hh * HEAD_SLAB, (hh + 1) * HEAD_SLAB)
            o_ref[:, lanes] = _da_finish(state[hh]["l"], state[hh]["acc"], tq, lam_ref[0, 0], lam_init,
                                         subln_ref[...], sg_ref[:, lanes])

    _per_q_block(pl.program_id(2), nq, body)


def _da_prompt(q, kt, v, sg, bias, table, lam, subln, batch, seq, tq, lam_init):
    t, width = q.shape
    hps = 2
    nh = width // HEAD_SLAB // hps
    nq = seq // tq
    w = hps * HEAD_SLAB
    return pl.pallas_call(
        functools.partial(_da_prompt_kernel, tq=tq, tkf=2 * tq, nq=nq, lam_init=lam_init, hps=hps),
        grid=(batch, nh, nq),
        in_specs=[
            pl.BlockSpec(memory_space=pltpu.SMEM),
            pl.BlockSpec(memory_space=pltpu.SMEM),
            pl.BlockSpec((tq, w), lambda b, h, i: (b * nq + i, h)),
            pl.BlockSpec((1, w, seq), lambda b, h, i: (b, h, 0)),
            pl.BlockSpec((seq, w), lambda b, h, i: (b, h)),
            pl.BlockSpec((tq, w), lambda b, h, i: (b * nq + i, h)),
            pl.BlockSpec((2 * hps, 2, tq, tq), lambda b, h, i: (h, 0, 0, 0)),
            pl.BlockSpec((1, HEAD_SLAB), lambda b, h, i: (0, 0)),
        ],
        out_specs=pl.BlockSpec((tq, w), lambda b, h, i: (b * nq + i, h)),
        out_shape=jax.ShapeDtypeStruct((t, width), BF16),
        compiler_params=_cparams(("parallel", "parallel", "arbitrary")),
        name="da_prompt",
    )(table, lam, q, kt, v, sg, bias, subln.reshape(1, HEAD_SLAB))


def _da_sample_kernel(lam_ref, q_ref, kct_ref, vc_ref, kn_ref, vn_ref, sg_ref, bc_ref, bn_ref,
                      subln_ref, o_ref, *, t, past, lam_init, hps):
    qpos = past + lax.broadcasted_iota(jnp.int32, (2 * t, 1), 0) % t
    vis_n = ((past + lax.broadcasted_iota(jnp.int32, (1, t), 1)) // CHUNK) <= (qpos // CHUNK)
    vis_c = (lax.broadcasted_iota(jnp.int32, (1, past), 1) // CHUNK) <= (qpos // CHUNK)
    scores = []
    for hh in range(hps):
        lanes = slice(hh * HEAD_SLAB, (hh + 1) * HEAD_SLAB)
        qq = _split_halves(q_ref[:, lanes])
        sn = lax.dot_general(qq, kn_ref[:, lanes], _NT, preferred_element_type=F32)
        sn = jnp.where(vis_n, sn + bn_ref[2 * hh:2 * hh + 2].reshape(2 * t, t), NEG_INF)
        sc = jnp.dot(qq, kct_ref[0, lanes, :].astype(BF16), preferred_element_type=F32)
        sc = jnp.where(vis_c, sc + bc_ref[2 * hh:2 * hh + 2].reshape(2 * t, past), NEG_INF)
        scores.append((sn, sc))
    for hh, (sn, sc) in enumerate(scores):
        lanes = slice(hh * HEAD_SLAB, (hh + 1) * HEAD_SLAB)
        m = jnp.maximum(jnp.max(sn, axis=-1, keepdims=True), jnp.max(sc, axis=-1, keepdims=True))
        pn = jnp.exp2(sn - m)
        pc = jnp.exp2(sc - m)
        l = jnp.sum(pn, axis=-1, keepdims=True) + jnp.sum(pc, axis=-1, keepdims=True)
        acc = (jnp.dot(pn.astype(BF16), vn_ref[:, lanes], preferred_element_type=F32)
               + jnp.dot(pc.astype(BF16), vc_ref[0, :, lanes].astype(BF16), preferred_element_type=F32))
        o_ref[:, lanes] = _da_finish(l, acc, t, lam_ref[0, 0], lam_init, subln_ref[...], sg_ref[:, lanes])


def _da_sample(q, ckt, cv, kn, vn, sg, bias_c, bias_n, lam, subln, batch, t, lam_init):
    tt, width = q.shape
    hps = 4
    nh = width // HEAD_SLAB // hps
    w = hps * HEAD_SLAB
    past = cv.shape[1]
    return pl.pallas_call(
        functools.partial(_da_sample_kernel, t=t, past=past, lam_init=lam_init, hps=hps),
        grid=(batch, nh),
        in_specs=[
            pl.BlockSpec(memory_space=pltpu.SMEM),
            pl.BlockSpec((t, w), lambda b, h: (b, h)),
            pl.BlockSpec((1, w, past), lambda b, h: (b, h, 0)),
            pl.BlockSpec((1, past, w), lambda b, h: (b, 0, h)),
            pl.BlockSpec((t, w), lambda b, h: (b, h)),
            pl.BlockSpec((t, w), lambda b, h: (b, h)),
            pl.BlockSpec((t, w), lambda b, h: (b, h)),
            pl.BlockSpec((2 * hps, t, past), lambda b, h: (h, 0, 0)),
            pl.BlockSpec((2 * hps, t, t), lambda b, h: (h, 0, 0)),
            pl.BlockSpec((1, HEAD_SLAB), lambda b, h: (0, 0)),
        ],
        out_specs=pl.BlockSpec((t, w), lambda b, h: (b, h)),
        out_shape=jax.ShapeDtypeStruct((tt, width), BF16),
        compiler_params=_cparams(("parallel", "parallel")),
        name="da_sample",
    )(lam, q, ckt, cv, kn, vn, sg, bias_c, bias_n, subln.reshape(1, HEAD_SLAB))


def _suffix_matrix(n):
    j = np.arange(n)[:, None]
    s = np.arange(n)[None, :]
    return jnp.asarray((j > s).astype(np.float32), dtype=BF16)


def _sb_logs(z, mask):
    lg = jnp.log(1.0 + jnp.exp2(-jnp.abs(z))) * LOG2E
    ls = jnp.minimum(z, 0.0) - lg
    lr = ls - z
    if mask is not None:
        lr = jnp.where(mask, lr, 0.0)
    return ls, lr


def _sb_suffix(lr, tmat):
    hi = lr.astype(BF16)
    lo = (lr - hi.astype(F32)).astype(BF16)
    return (jnp.dot(hi, tmat, preferred_element_type=F32)
            + jnp.dot(lo, tmat, preferred_element_type=F32))


def _sb_exp(ls, after, c, mask):
    arg = ls + after
    if c is not None:
        arg = arg + c
    a = jnp.exp2(arg)
    if mask is not None:
        a = jnp.where(mask, a, 0.0)
    return a.astype(BF16)


SB_DEAD_LOG2 = -160.0


def _sb_prompt_kernel(q_ref, kt_ref, vt_ref, sg_ref, tmat_ref, o_ref, acc_scr, rem_scr, *, tq, nq):
    qq = _split_halves(q_ref[...])
    tmat = tmat_ref[...]
    r = lax.broadcasted_iota(jnp.int32, (2 * tq, tq), 0) % tq
    c = lax.broadcasted_iota(jnp.int32, (2 * tq, tq), 1)

    def body(qv):
        starts = [w * tq for w in range(qv, -1, -1)]

        def scores(w):
            return jnp.dot(qq, kt_ref[0, :, starts[w]:starts[w] + tq], preferred_element_type=F32)

        def finish(pending, acc):
            ls, after, rem, mask, w = pending
            a = _sb_exp(ls, after, rem, mask)
            pv = lax.dot_general(a, vt_ref[0, :, starts[w]:starts[w] + tq], _NT, preferred_element_type=F32)
            return pv if acc is None else acc + pv

        def run(ws, acc, rem):
            zs = {w: scores(w) for w in ws[:2]}
            pending = None
            for i, w in enumerate(ws):
                mask = (c < r) if w == 0 else None
                ls, lr = _sb_logs(zs.pop(w), mask)
                if i + 2 < len(ws):
                    zs[ws[i + 2]] = scores(ws[i + 2])
                if pending is not None:
                    acc = finish(pending, acc)
                pending = (ls, _sb_suffix(lr, tmat), rem, mask, w)
                tot = jnp.sum(lr, axis=-1, keepdims=True)
                rem = tot if rem is None else rem + tot
            return finish(pending, acc), rem

        ws = list(range(len(starts)))
        acc, rem = run(ws[:2], None, None)
        if len(ws) > 2:
            acc_scr[...] = acc
            rem_scr[...] = rem

            @pl.when(jnp.max(rem) > SB_DEAD_LOG2)
            def _():
                acc_scr[...] = run(ws[2:], acc_scr[...], rem_scr[...])[0]

            acc = acc_scr[...]
        o_ref[...] = (sg_ref[...].astype(F32) * _join_halves(acc, tq)).astype(BF16)

    _per_q_block(pl.program_id(2), nq, body)


def _sb_prompt(q, kt, vt, sg, batch, seq, tq):
    t, width = q.shape
    nh = width // HEAD_SLAB
    nq = seq // tq
    return pl.pallas_call(
        functools.partial(_sb_prompt_kernel, tq=tq, nq=nq),
        grid=(batch, nh, nq),
        in_specs=[
            pl.BlockSpec((tq, HEAD_SLAB), lambda b, h, i: (b * nq + i, h)),
            pl.BlockSpec((1, HEAD_SLAB, seq), lambda b, h, i: (b, h, 0)),
            pl.BlockSpec((1, HEAD_SLAB, seq), lambda b, h, i: (b, h, 0)),
            pl.BlockSpec((tq, HEAD_SLAB), lambda b, h, i: (b * nq + i, h)),
            pl.BlockSpec((tq, tq), lambda b, h, i: (0, 0)),
        ],
        out_specs=pl.BlockSpec((tq, HEAD_SLAB), lambda b, h, i: (b * nq + i, h)),
        out_shape=jax.ShapeDtypeStruct((t, width), BF16),
        scratch_shapes=[pltpu.VMEM((2 * tq, HEAD_SLAB), F32), pltpu.VMEM((2 * tq, 1), F32)],
        compiler_params=_cparams(("parallel", "parallel", "arbitrary")),
        name="sb_prompt",
    )(q, kt, vt, sg, _suffix_matrix(tq))


def _sb_sample_kernel(q_ref, kct_ref, vct_ref, kn_ref, vn_ref, sg_ref, tmat_ref, o_ref, *, t, tn, tk, past, hps):
    tmat = tmat_ref[...]
    pad = jnp.zeros((tn - t, HEAD_SLAB), BF16)
    r = lax.broadcasted_iota(jnp.int32, (2 * t, tn), 0) % t
    c = lax.broadcasted_iota(jnp.int32, (2 * t, tn), 1)
    blocks = []
    for hh in range(hps):
        lanes = slice(hh * HEAD_SLAB, (hh + 1) * HEAD_SLAB)
        qq = _split_halves(q_ref[:, lanes])
        kn = jnp.concatenate([kn_ref[:, lanes], pad], axis=0)
        z = lax.dot_general(qq, kn, _NT, preferred_element_type=F32)
        blocks.append(dict(hh=hh, z=z, mask=c < r, tm=tmat[:tn, :tn], j=None))
        for j in reversed(range(past // tk)):
            z = jnp.dot(qq, kct_ref[0, lanes, j * tk:(j + 1) * tk].astype(BF16), preferred_element_type=F32)
            blocks.append(dict(hh=hh, z=z, mask=None, tm=tmat, j=j))
    rem = [None] * hps
    for bk in blocks:
        bk["ls"], bk["lr"] = _sb_logs(bk.pop("z"), bk["mask"])
        bk["rem"] = rem[bk["hh"]]
        tot = jnp.sum(bk["lr"], axis=-1, keepdims=True)
        rem[bk["hh"]] = tot if bk["rem"] is None else bk["rem"] + tot
    for bk in blocks:
        bk["after"] = _sb_suffix(bk.pop("lr"), bk["tm"])
    acc = [None] * hps
    for bk in blocks:
        hh, j = bk["hh"], bk["j"]
        lanes = slice(hh * HEAD_SLAB, (hh + 1) * HEAD_SLAB)
        a = _sb_exp(bk["ls"], bk["after"], bk["rem"], bk["mask"])
        if j is None:
            vn = jnp.concatenate([vn_ref[:, lanes], pad], axis=0)
            pv = jnp.dot(a, vn, preferred_element_type=F32)
        else:
            vj = vct_ref[0, lanes, j * tk:(j + 1) * tk].astype(BF16)
            pv = lax.dot_general(a, vj, _NT, preferred_element_type=F32)
        acc[hh] = pv if acc[hh] is None else acc[hh] + pv
    for hh in range(hps):
        lanes = slice(hh * HEAD_SLAB, (hh + 1) * HEAD_SLAB)
        o_ref[:, lanes] = (sg_ref[:, lanes].astype(F32) * _join_halves(acc[hh], t)).astype(BF16)


def _sb_sample(q, ckt, cvt, kn, vn, sg, batch, t, tk):
    tt, width = q.shape
    hps = 4
    nh = width // HEAD_SLAB // hps
    w = hps * HEAD_SLAB
    past = ckt.shape[2]
    tk = min(tk, past)
    assert past % tk == 0
    tn = HEAD_SLAB
    assert t <= tn <= tk
    return pl.pallas_call(
        functools.partial(_sb_sample_kernel, t=t, tn=tn, tk=tk, past=past, hps=hps),
        grid=(batch, nh),
        in_specs=[
            pl.BlockSpec((t, w), lambda b, h: (b, h)),
            pl.BlockSpec((1, w, past), lambda b, h: (b, h, 0)),
            pl.BlockSpec((1, w, past), lambda b, h: (b, h, 0)),
            pl.BlockSpec((t, w), lambda b, h: (b, h)),
            pl.BlockSpec((t, w), lambda b, h: (b, h)),
            pl.BlockSpec((t, w), lambda b, h: (b, h)),
            pl.BlockSpec((tk, tk), lambda b, h: (0, 0)),
        ],
        out_specs=pl.BlockSpec((t, w), lambda b, h: (b, h)),
        out_shape=jax.ShapeDtypeStruct((tt, width), BF16),
        compiler_params=_cparams(("parallel", "parallel")),
        name="sb_sample",
    )(q, ckt, cvt, kn, vn, sg, _suffix_matrix(tk))


def _sw_group(sink_ref, q_slabs, kk, vv, bias, valid, kvh, group, t):
    qst = jnp.concatenate([_split_halves(qs) for qs in q_slabs], axis=0)
    s = lax.dot_general(qst, kk, _NT, preferred_element_type=F32) + bias
    s = jnp.where(valid, s, NEG_INF)
    row = lax.broadcasted_iota(jnp.int32, (group * t, 1), 0)
    sk = jnp.zeros((group * t, 1), F32)
    for g in range(group):
        sk = jnp.where(row // t == g, sink_ref[kvh * group + g], sk)
    m = jnp.maximum(jnp.max(s, axis=-1, keepdims=True), sk)
    p = jnp.exp(s - m)
    w = p / (jnp.sum(p, axis=-1, keepdims=True) + jnp.exp(sk - m))
    o = jnp.dot(w.astype(BF16), vv, preferred_element_type=F32)
    return [_join_halves(o[2 * i * t:(2 * i + 2) * t], t) for i in range(group // 2)]


def _sw_prompt_kernel(sink_ref, q_ref, kp_ref, kc_ref, vp_ref, vc_ref, sg_ref, bias_ref, o_ref,
                      *, tq, n_kv, group):
    qi = pl.program_id(1)
    band = (WIN_CHUNKS + 1) * CHUNK
    kcat = jnp.concatenate([kp_ref[...], kc_ref[...]], axis=0)
    vcat = jnp.concatenate([vp_ref[...], vc_ref[...]], axis=0)
    col = lax.broadcasted_iota(jnp.int32, (1, band), 1)
    for cc in range(tq // CHUNK):
        r0 = cc * CHUNK
        valid = (col >= WINDOW - r0) | (qi > 0)
        for kvh in range(n_kv):
            kk = kcat[r0:r0 + band, kvh * HEAD_SLAB:(kvh + 1) * HEAD_SLAB]
            vv = vcat[r0:r0 + band, kvh * HEAD_SLAB:(kvh + 1) * HEAD_SLAB]
            s0 = kvh * (group // 2)
            q_slabs = [q_ref[r0:r0 + CHUNK, (s0 + i) * HEAD_SLAB:(s0 + i + 1) * HEAD_SLAB]
                       for i in range(group // 2)]
            bias = bias_ref[kvh * group:(kvh + 1) * group].reshape(group * CHUNK, band)
            outs = _sw_group(sink_ref, q_slabs, kk, vv, bias, valid, kvh, group, CHUNK)
            for i, o in enumerate(outs):
                lanes = slice((s0 + i) * HEAD_SLAB, (s0 + i + 1) * HEAD_SLAB)
                sg = sg_ref[r0:r0 + CHUNK, lanes].astype(F32)
                o_ref[r0:r0 + CHUNK, lanes] = (sg * o).astype(BF16)


def _sw_prompt(q, kd, vd, sg, bias, sinks, batch, seq, tq, n_kv, group):
    t, width = q.shape
    nq = seq // tq
    wpb = tq // WINDOW
    kvw = kd.shape[1]
    band = (WIN_CHUNKS + 1) * CHUNK
    prev = lambda b, i: (jnp.maximum((b * nq + i) * wpb - 1, 0), 0)
    cur = lambda b, i: (b * nq + i, 0)
    return pl.pallas_call(
        functools.partial(_sw_prompt_kernel, tq=tq, n_kv=n_kv, group=group),
        grid=(batch, nq),
        in_specs=[
            pl.BlockSpec(memory_space=pltpu.SMEM),
            pl.BlockSpec((tq, width), cur),
            pl.BlockSpec((WINDOW, kvw), prev),
            pl.BlockSpec((tq, kvw), cur),
            pl.BlockSpec((WINDOW, kvw), prev),
            pl.BlockSpec((tq, kvw), cur),
            pl.BlockSpec((tq, width), cur),
            pl.BlockSpec((n_kv * group, CHUNK, band), lambda b, i: (0, 0, 0)),
        ],
        out_specs=pl.BlockSpec((tq, width), cur),
        out_shape=jax.ShapeDtypeStruct((t, width), BF16),
        compiler_params=_cparams(("parallel", "arbitrary")),
        name="sw_prompt",
    )(sinks, q, kd, kd, vd, vd, sg, bias)


def _sw_sample_kernel(sink_ref, q_ref, k_ref, v_ref, sg_ref, bias_ref, o_ref, *, t, past, wb, n_kv, group):
    band = wb + t
    qpos = past + lax.broadcasted_iota(jnp.int32, (group * t, 1), 0) % t
    kpos = past - wb + lax.broadcasted_iota(jnp.int32, (1, band), 1)
    qc = qpos // CHUNK
    kc = kpos // CHUNK
    valid = (kc <= qc) & (kc >= qc - WIN_CHUNKS)
    for kvh in range(n_kv):
        kk = k_ref[0, :, kvh * HEAD_SLAB:(kvh + 1) * HEAD_SLAB]
        vv = v_ref[0, :, kvh * HEAD_SLAB:(kvh + 1) * HEAD_SLAB]
        s0 = kvh * (group // 2)
        q_slabs = [q_ref[:, (s0 + i) * HEAD_SLAB:(s0 + i + 1) * HEAD_SLAB] for i in range(group // 2)]
        bias = bias_ref[kvh * group:(kvh + 1) * group].reshape(group * t, band)
        outs = _sw_group(sink_ref, q_slabs, kk, vv, bias, valid, kvh, group, t)
        for i, o in enumerate(outs):
            lanes = slice((s0 + i) * HEAD_SLAB, (s0 + i + 1) * HEAD_SLAB)
            o_ref[:, lanes] = (sg_ref[:, lanes].astype(F32) * o).astype(BF16)


def _sw_sample(q, kd, vd, sg, bias, sinks, batch, t, past, wb, n_kv, group):
    tt, width = q.shape
    band, kvw = kd.shape[1:]
    return pl.pallas_call(
        functools.partial(_sw_sample_kernel, t=t, past=past, wb=wb, n_kv=n_kv, group=group),
        grid=(batch,),
        in_specs=[
            pl.BlockSpec(memory_space=pltpu.SMEM),
            pl.BlockSpec((t, width), lambda b: (b, 0)),
            pl.BlockSpec((1, band, kvw), lambda b: (b, 0, 0)),
            pl.BlockSpec((1, band, kvw), lambda b: (b, 0, 0)),
            pl.BlockSpec((t, width), lambda b: (b, 0)),
            pl.BlockSpec((n_kv * group, t, band), lambda b: (0, 0, 0)),
        ],
        out_specs=pl.BlockSpec((t, width), lambda b: (b, 0)),
        out_shape=jax.ShapeDtypeStruct((tt, width), BF16),
        compiler_params=_cparams(("parallel",)),
        name="sw_sample",
    )(sinks, q, kd, vd, sg, bias)


TM_PROJ = 256
TQ_ATTN = 256


def _feature_major(a):
    nd = a.ndim
    a = jnp.transpose(a, (0,) + tuple(range(2, nd)) + (1,))
    return a.reshape(a.shape[0], -1, a.shape[-1])


def _position_major(a, feat_shape):
    b, _, p = a.shape
    a = a.reshape((b,) + tuple(feat_shape) + (p,))
    nd = a.ndim
    return jnp.transpose(a, (0, nd - 1) + tuple(range(1, nd - 1)))


def _project(x, res, norm, w_nat, plan, dtypes, tm, **kw):
    outs = _inproj(x, norm, w_nat, plan, dtypes, tm, res=res, **kw)
    return (x, outs) if res is None else (outs[0], outs[1:])


def _da_layer(xp, xs, res_p, res_s, ck, cv, table, norm, w_in, lam_params, subln, lam_init, dims):
    batch, seq, dec_batch, t_dec, past, d = dims
    br = d
    wq, wk, wv, wg = (w_in[:, i * br:(i + 1) * br] for i in range(4))
    w_p = jnp.concatenate([wq, wv, wg], axis=1).astype(BF16)
    plan_p = [(0, br, "qscale2"), (br, br, "none"), (2 * br, br, "silu"), (br, br, "none")]
    xp, (qp, vp, sgp, vpf, ktf, kt) = _project(
        xp, res_p, norm, w_p, plan_p, [BF16, BF16, BF16, F32], TM_PROJ,
        w_t=wk.T.astype(BF16), plan_t=[(0, br), (0, br)], dtypes_t=[F32, BF16], seq=seq)
    plan_s = [(0, br, "qscale2"), (br, br, "none"), (2 * br, br, "none"), (3 * br, br, "silu"),
              (br, br, "none"), (2 * br, br, "none")]
    xs, (qs, ks, vs, sgs, ksf, vsf) = _project(xs, res_s, norm, w_in.astype(BF16), plan_s,
                                               [BF16, BF16, BF16, BF16, F32, F32], TM_PROJ)
    lam = _diff_lambda(lam_params, lam_init)
    tq = min(TQ_ATTN, seq)
    assert seq % tq == 0 and tq >= MAX_DISTANCE and tq % CHUNK == 0
    assert past == ck.shape[1] and past % CHUNK == 0 and t_dec <= CHUNK
    i = jnp.arange(tq)[:, None]
    j = jnp.arange(tq)[None, :]
    rel_p = jnp.concatenate([j - i, j - tq - i], axis=0)
    nbh = table.shape[1]
    bias_p = _bias_tiles(table, rel_p, LOG2E).reshape(nbh, 2, tq, tq)
    og_p = _da_prompt(qp, kt, vp, sgp, bias_p, table, lam, subln, batch, seq, tq, lam_init)
    q_pos = past + jnp.arange(t_dec)[:, None]
    bias_c = _bias_tiles(table, jnp.arange(past)[None, :] - q_pos, LOG2E)
    bias_n = _bias_tiles(table, past + jnp.arange(t_dec)[None, :] - q_pos, LOG2E)
    og_s = _da_sample(qs, _feature_major(ck), cv.reshape(dec_batch, past, br), ks, vs, sgs,
                      bias_c, bias_n, lam, subln, dec_batch, t_dec, lam_init)
    state = (_position_major(ktf, ck.shape[2:]), vpf.reshape((batch, seq) + cv.shape[2:]),
             ksf.reshape((dec_batch, t_dec) + ck.shape[2:]), vsf.reshape((dec_batch, t_dec) + cv.shape[2:]))
    return xp, xs, og_p, og_s, state


def _sb_layer(xp, xs, res_p, res_s, ck, cv, norm, w_in, dims):
    batch, seq, dec_batch, t_dec, past, d = dims
    br = d
    wq, wk, wv, wg = (w_in[:, i * br:(i + 1) * br] for i in range(4))
    w_p = jnp.concatenate([wq, wg], axis=1).astype(BF16)
    w_t = jnp.concatenate([wk, wv], axis=1).T.astype(BF16)
    xp, (qp, sgp, ktf, kt, vtf, vt) = _project(
        xp, res_p, norm, w_p, [(0, br, "qscale2"), (br, br, "silu")], [BF16, BF16], TM_PROJ,
        w_t=w_t, plan_t=[(0, br), (0, br), (br, br), (br, br)], dtypes_t=[F32, BF16, F32, BF16], seq=seq)
    plan_s = [(0, br, "qscale2"), (br, br, "none"), (2 * br, br, "none"), (3 * br, br, "silu"),
              (br, br, "none"), (2 * br, br, "none")]
    xs, (qs, ks, vs, sgs, ksf, vsf) = _project(xs, res_s, norm, w_in.astype(BF16), plan_s,
                                               [BF16, BF16, BF16, BF16, F32, F32], TM_PROJ)
    tq = min(TQ_ATTN, seq)
    assert seq % tq == 0 and past == ck.shape[1]
    og_p = _sb_prompt(qp, kt, vt, sgp, batch, seq, tq)
    og_s = _sb_sample(qs, _feature_major(ck), _feature_major(cv), ks, vs, sgs, dec_batch, t_dec, TQ_ATTN)
    state = (_position_major(ktf, ck.shape[2:]), _position_major(vtf, cv.shape[2:]),
             ksf.reshape((dec_batch, t_dec) + ck.shape[2:]), vsf.reshape((dec_batch, t_dec) + cv.shape[2:]))
    return xp, xs, og_p, og_s, state


def _dup_heads(a, n_kv, hd):
    lead = a.shape[:-1]
    a = a.reshape(lead + (n_kv, 1, hd))
    return jnp.broadcast_to(a, lead + (n_kv, 2, hd)).reshape(lead + (n_kv * 2 * hd,))


def _sw_layer(xp, xs, res_p, res_s, ck, cv, table, norm, w_in, sinks, dims):
    batch, seq, dec_batch, t_dec, past, d = dims
    br = d
    wb, n_kv, hd = ck.shape[1:]
    assert hd == HALF
    kvw = n_kv * hd
    group = br // hd // n_kv
    assert group % 2 == 0
    wq, wk, wv, wg = (w_in[:, :br], w_in[:, br:br + kvw], w_in[:, br + kvw:br + 2 * kvw],
                      w_in[:, br + 2 * kvw:])
    w_p = jnp.concatenate([wq, _dup_heads(wk, n_kv, hd), _dup_heads(wv, n_kv, hd), wg], axis=1).astype(BF16)
    plan_p = [(0, br, "qscale"), (br, 2 * kvw, "none"), (br + 2 * kvw, 2 * kvw, "none"),
              (br + 4 * kvw, br, "silu")]
    xp, (qp, kdp, vdp, sgp) = _project(xp, res_p, norm, w_p, plan_p, [BF16] * 4, TM_PROJ)
    keep = min(WINDOW, seq)
    w_kv_t = jnp.concatenate([wk, wv], axis=1).T.astype(BF16)
    x_tail = xp.reshape(batch, seq, d)[:, seq - keep:].reshape(batch * keep, d)
    ktf, vtf = _inproj(x_tail, norm, None, [], [], keep, w_t=w_kv_t,
                       plan_t=[(0, kvw), (kvw, kvw)], dtypes_t=[F32, F32], seq=keep)
    plan_s = [(0, br, "qscale"), (br, kvw, "none"), (br + kvw, kvw, "none"), (br + 2 * kvw, br, "silu")]
    xs, (qs, ksf, vsf, sgs) = _project(xs, res_s, norm, w_in.astype(BF16), plan_s, [BF16, F32, F32, BF16],
                                       TM_PROJ)

    tq = min(TQ_ATTN, seq)
    assert seq % tq == 0 and tq % WINDOW == 0
    band = (WIN_CHUNKS + 1) * CHUNK
    rel_p = (jnp.arange(band)[None, :] - WIN_CHUNKS * CHUNK) - jnp.arange(CHUNK)[:, None]
    bias_p = _bias_tiles(table, rel_p)
    og_p = _sw_prompt(qp, kdp, vdp, sgp, bias_p, sinks, batch, seq, tq, n_kv, group)

    k_all = jnp.concatenate([ck, ksf.reshape(dec_batch, t_dec, n_kv, hd)], axis=1)
    v_all = jnp.concatenate([cv, vsf.reshape(dec_batch, t_dec, n_kv, hd)], axis=1)
    kd_s = _dup_heads(k_all.reshape(dec_batch, wb + t_dec, kvw), n_kv, hd).astype(BF16)
    vd_s = _dup_heads(v_all.reshape(dec_batch, wb + t_dec, kvw), n_kv, hd).astype(BF16)
    rel_s = (past - wb + jnp.arange(wb + t_dec)[None, :]) - (past + jnp.arange(t_dec)[:, None])
    bias_s = _bias_tiles(table, rel_s)
    og_s = _sw_sample(qs, kd_s, vd_s, sgs, bias_s, sinks, dec_batch, t_dec, past, wb, n_kv, group)
    state = (_position_major(ktf, (n_kv, hd)), _position_major(vtf, (n_kv, hd)),
             k_all[:, -wb:], v_all[:, -wb:])
    return xp, xs, og_p, og_s, state


def kernel(x_prompt, x_sample, cache_k_0, cache_v_0, cache_k_1, cache_v_1, cache_k_2, cache_v_2,
           cache_k_3, cache_v_3, rel_bias_table,
           norm_0, w_in_0, w_out_0, da_lambda_0, da_subln_0,
           norm_1, w_in_1, w_out_1,
           norm_2, w_in_2, w_out_2, sw_sinks_2,
           norm_3, w_in_3, w_out_3, da_lambda_3, da_subln_3,
           final_norm):
    batch, seq, d = x_prompt.shape
    dec_batch, t_dec, _ = x_sample.shape
    dims = (batch, seq, dec_batch, t_dec, cache_k_0.shape[1], d)
    xp = x_prompt.reshape(batch * seq, d)
    xs = x_sample.reshape(dec_batch * t_dec, d)
    layers = [
        ("da", cache_k_0, cache_v_0, norm_0, w_in_0, w_out_0, (da_lambda_0, da_subln_0)),
        ("sb", cache_k_1, cache_v_1, norm_1, w_in_1, w_out_1, ()),
        ("sw", cache_k_2, cache_v_2, norm_2, w_in_2, w_out_2, (sw_sinks_2,)),
        ("da", cache_k_3, cache_v_3, norm_3, w_in_3, w_out_3, (da_lambda_3, da_subln_3)),
    ]
    states = []
    res_p = res_s = None
    for i, (kind, ck, cv, norm, w_in, w_out, extra) in enumerate(layers):
        if kind == "da":
            lam_init = 0.8 - 0.6 * math.exp(-0.3 * i)
            xp, xs, og_p, og_s, st = _da_layer(xp, xs, res_p, res_s, ck, cv, rel_bias_table, norm, w_in,
                                               extra[0], extra[1], lam_init, dims)
        elif kind == "sb":
            xp, xs, og_p, og_s, st = _sb_layer(xp, xs, res_p, res_s, ck, cv, norm, w_in, dims)
        else:
            xp, xs, og_p, og_s, st = _sw_layer(xp, xs, res_p, res_s, ck, cv, rel_bias_table, norm, w_in,
                                               extra[0], dims)
        w_o = w_out.astype(BF16)
        res_p, res_s = (og_p, w_o), (og_s, w_o)
        states.append(st)
    xp = _outproj(xp, res_p[0], res_p[1], final_norm, TM_PROJ)
    xs = _outproj(xs, res_s[0], res_s[1], final_norm, TM_PROJ)
    out = [xp.reshape(batch, seq, d), xs.reshape(dec_batch, t_dec, d)]
    for st in states:
        out.extend(st)
    return tuple(out)
```

```python
import functools
import math

import jax
import jax.numpy as jnp
import numpy as np
from jax import lax
from jax.experimental import pallas as pl
from jax.experimental.pallas import tpu as pltpu

F32 = jnp.float32
BF16 = jnp.bfloat16

EPS = 1e-6
NEG_INF = -1e30
LOG2E = math.log2(math.e)
CHUNK = 64
WINDOW = 128
WIN_CHUNKS = WINDOW // CHUNK
N_BUCKETS = 32
MAX_DISTANCE = 128
FAR_BUCKET = N_BUCKETS // 2 - 1
HEAD_SLAB = 128
HALF = HEAD_SLAB // 2
VMEM_LIMIT = 48 * 1024 * 1024

_NT = (((1,), (1,)), ((), ()))


def _cparams(sem):
    return pltpu.CompilerParams(dimension_semantics=sem, vmem_limit_bytes=VMEM_LIMIT)


def _inproj_kernel(*refs, plan, plan_t, has_nat, has_res):
    x_ref, refs = refs[0], refs[1:]
    x = x_ref[...]
    if has_res:
        (og_ref, wo_ref), refs = refs[:2], refs[2:]
        x = x + jnp.dot(og_ref[...], wo_ref[...], preferred_element_type=F32)
    g_ref, refs = refs[0], refs[1:]
    if has_nat:
        w_ref, refs = refs[0], refs[1:]
    if plan_t:
        wt_ref, refs = refs[0], refs[1:]
    if has_res:
        refs[0][...] = x
        refs = refs[1:]
    y = x * lax.rsqrt(jnp.mean(x * x, axis=-1, keepdims=True) + EPS)
    xn = (y * g_ref[...]).astype(BF16)
    if has_nat:
        h = jnp.dot(xn, w_ref[...], preferred_element_type=F32)
        for o_ref, (c0, width, kind) in zip(refs, plan):
            t = h[:, c0:c0 + width]
            if kind == "qscale2":
                t = t * (HALF ** -0.5 * LOG2E)
            elif kind == "silu":
                t = t * jax.nn.sigmoid(t)
            o_ref[...] = t.astype(o_ref.dtype)
        refs = refs[len(plan):]
    if plan_t:
        ht = lax.dot_general(wt_ref[...], xn, _NT, preferred_element_type=F32)
        for o_ref, (r0, nrows) in zip(refs, plan_t):
            o_ref[0] = ht[r0:r0 + nrows].astype(o_ref.dtype)


def _inproj(x2d, g, w_nat, plan, dtypes, tm, w_t=None, plan_t=(), dtypes_t=(), seq=None, res=None):
    t, d = x2d.shape
    tm = min(tm, t)
    assert t % tm == 0
    in_specs = [pl.BlockSpec((tm, d), lambda i: (i, 0))]
    args = [x2d]
    out_shape, out_specs = [], []
    if res is not None:
        og, w_out = res
        in_specs += [pl.BlockSpec((tm, og.shape[1]), lambda i: (i, 0)), pl.BlockSpec(w_out.shape, lambda i: (0, 0))]
        args += [og, w_out]
        out_shape.append(jax.ShapeDtypeStruct((t, d), F32))
        out_specs.append(pl.BlockSpec((tm, d), lambda i: (i, 0)))
    in_specs.append(pl.BlockSpec((1, d), lambda i: (0, 0)))
    args.append(g.reshape(1, d))
    if w_nat is not None:
        in_specs.append(pl.BlockSpec(w_nat.shape, lambda i: (0, 0)))
        args.append(w_nat)
        out_shape += [jax.ShapeDtypeStruct((t, width), dt) for (_, width, _), dt in zip(plan, dtypes)]
        out_specs += [pl.BlockSpec((tm, width), lambda i: (i, 0)) for (_, width, _) in plan]
    if plan_t:
        assert seq % tm == 0 and t % seq == 0
        tps = seq // tm
        in_specs.append(pl.BlockSpec(w_t.shape, lambda i: (0, 0)))
        args.append(w_t)
        out_shape += [jax.ShapeDtypeStruct((t // seq, nrows, seq), dt) for (_, nrows), dt in zip(plan_t, dtypes_t)]
        out_specs += [pl.BlockSpec((1, nrows, tm), lambda i: (i // tps, 0, i % tps)) for (_, nrows) in plan_t]
    return pl.pallas_call(
        functools.partial(_inproj_kernel, plan=tuple(plan), plan_t=tuple(plan_t), has_nat=w_nat is not None,
                          has_res=res is not None),
        grid=(t // tm,),
        in_specs=in_specs,
        out_specs=out_specs,
        out_shape=out_shape,
        compiler_params=_cparams(("parallel",)),
        name="inproj",
    )(*args)


def _outproj_kernel(x_ref, og_ref, w_ref, g_ref, o_ref):
    x = x_ref[...] + jnp.dot(og_ref[...], w_ref[...], preferred_element_type=F32)
    y = x * lax.rsqrt(jnp.mean(x * x, axis=-1, keepdims=True) + EPS)
    o_ref[...] = y * g_ref[...]


def _outproj(x2d, og, w_bf16, final_g, tm):
    t, d = x2d.shape
    br = og.shape[1]
    tm = min(tm, t)
    assert t % tm == 0
    return pl.pallas_call(
        _outproj_kernel,
        grid=(t // tm,),
        in_specs=[
            pl.BlockSpec((tm, d), lambda i: (i, 0)),
            pl.BlockSpec((tm, br), lambda i: (i, 0)),
            pl.BlockSpec((br, d), lambda i: (0, 0)),
            pl.BlockSpec((1, d), lambda i: (0, 0)),
        ],
        out_specs=pl.BlockSpec((tm, d), lambda i: (i, 0)),
        out_shape=jax.ShapeDtypeStruct((t, d), F32),
        compiler_params=_cparams(("parallel",)),
        name="outproj",
    )(x2d, og, w_bf16, final_g.reshape(1, d))


def _rel_bucket(rel):
    nb = N_BUCKETS // 2
    max_exact = nb // 2
    n = jnp.abs(rel)
    nf = jnp.maximum(n, 1).astype(F32)
    large = max_exact + (jnp.log(nf / max_exact) / math.log(MAX_DISTANCE / max_exact)
                         * (nb - max_exact)).astype(jnp.int32)
    large = jnp.minimum(large, nb - 1)
    return jnp.where(rel > 0, nb, 0) + jnp.where(n < max_exact, n, large)


def _bias_kernel(tab_ref, idx_ref, o_ref, *, scale):
    h = pl.program_id(0)
    idx = idx_ref[...]
    acc = jnp.zeros(idx.shape, F32)
    for b in range(N_BUCKETS):
        acc = jnp.where(idx == b, tab_ref[b, h], acc)
    o_ref[0] = acc * scale if scale != 1.0 else acc


def _bias_tiles(table, rel, scale=1.0):
    idx = _rel_bucket(rel).astype(jnp.int32)
    r, c = idx.shape
    nh = table.shape[1]
    return pl.pallas_call(
        functools.partial(_bias_kernel, scale=scale),
        grid=(nh,),
        in_specs=[
            pl.BlockSpec(memory_space=pltpu.SMEM),
            pl.BlockSpec((r, c), lambda h: (0, 0)),
        ],
        out_specs=pl.BlockSpec((1, r, c), lambda h: (h, 0, 0)),
        out_shape=jax.ShapeDtypeStruct((nh, r, c), F32),
        compiler_params=_cparams(("arbitrary",)),
        name="bias_tiles",
    )(table, idx)


def _split_halves(q_bf16):
    qf = q_bf16.astype(F32)
    lane = lax.broadcasted_iota(jnp.int32, qf.shape, 1)
    qa = jnp.where(lane < HALF, qf, 0.0).astype(BF16)
    qb = jnp.where(lane >= HALF, qf, 0.0).astype(BF16)
    return jnp.concatenate([qa, qb], axis=0)


def _join_halves(o, t):
    lane = lax.broadcasted_iota(jnp.int32, (t, HEAD_SLAB), 1)
    return jnp.where(lane < HALF, o[:t], o[t:])


def _da_finish(l, acc, t, lam, lam_init, subln, sg):
    o = acc[:t] / l[:t] - lam * (acc[t:] / l[t:])
    o = o * lax.rsqrt(jnp.mean(o * o, axis=-1, keepdims=True) + EPS) * subln
    o = o * (1.0 - lam_init)
    return (sg.astype(F32) * o).astype(BF16)


def _lam_kernel(lp_ref, o_ref, *, lam_init):
    lp = lp_ref[...]
    a = jnp.sum(lp[0:1] * lp[1:2], axis=-1, keepdims=True)
    b = jnp.sum(lp[2:3] * lp[3:4], axis=-1, keepdims=True)
    o_ref[...] = jnp.exp(a) - jnp.exp(b) + lam_init


def _diff_lambda(lam_params, lam_init):
    return pl.pallas_call(
        functools.partial(_lam_kernel, lam_init=lam_init),
        out_shape=jax.ShapeDtypeStruct((1, 1), F32),
        name="diff_lambda",
    )(lam_params)


def _per_q_block(qi, nq, body):
    for qv in range(nq):
        pl.when(qi == qv)(functools.partial(body, qv))


def _da_prompt_kernel(tab_ref, lam_ref, q_ref, kt_ref, v_ref, sg_ref, bias_ref, subln_ref, o_ref,
                      *, tq, tkf, nq, lam_init, hps):
    hg = pl.program_id(1)
    rows = 2 * tq
    r = lax.broadcasted_iota(jnp.int32, (rows, tq), 0) % tq
    c = lax.broadcasted_iota(jnp.int32, (rows, tq), 1)
    row = lax.broadcasted_iota(jnp.int32, (rows, 1), 0)
    qqs, fars = [], []
    for hh in range(hps):
        h = hg * hps + hh
        qqs.append(_split_halves(q_ref[:, hh * HEAD_SLAB:(hh + 1) * HEAD_SLAB]))
        fars.append(LOG2E * jnp.where(row < tq, tab_ref[FAR_BUCKET, 2 * h], tab_ref[FAR_BUCKET, 2 * h + 1]))

    def body(qv):
        items = [(qv * tq, tq, "diag")]
        if qv >= 1:
            items.append(((qv - 1) * tq, tq, "prev"))
        nfar = max(qv - 1, 0) * tq
        for pos in range(0, nfar, tkf):
            items.append((pos, min(tkf, nfar - pos), "far"))
        work = [(hh, w) for w in range(len(items)) for hh in range(hps)]

        def scores(hh, w):
            st, n, kind = items[w]
            lanes = slice(hh * HEAD_SLAB, (hh + 1) * HEAD_SLAB)
            s = jnp.dot(qqs[hh], kt_ref[0, lanes, st:st + n], preferred_element_type=F32)
            if kind == "diag":
                s = s + bias_ref[2 * hh:2 * hh + 2, 0].reshape(rows, tq)
                s = jnp.where((c // CHUNK) <= (r // CHUNK), s, NEG_INF)
            elif kind == "prev":
                s = s + bias_ref[2 * hh:2 * hh + 2, 1].reshape(rows, tq)
            return s

        ahead = 2 * hps
        sc = {i: scores(*work[i]) for i in range(min(ahead, len(work)))}
        state = [dict(m=None, l=None, acc=None, far=False) for _ in range(hps)]
        for i, (hh, w) in enumerate(work):
            st, n, kind = items[w]
            z = state[hh]
            s = sc.pop(i)
            m = z["m"]
            if kind == "far" and not z["far"]:
                m = m - fars[hh]
                z["far"] = True
            bm = jnp.max(s, axis=-1, keepdims=True)
            if m is None:
                m_new = bm
                p = jnp.exp2(s - m_new)
                z["l"] = jnp.sum(p, axis=-1, keepdims=True)
            else:
                m_new = jnp.maximum(m, bm)
                alpha = jnp.exp2(m - m_new)
                p = jnp.exp2(s - m_new)
                z["l"] = alpha * z["l"] + jnp.sum(p, axis=-1, keepdims=True)
            if i + ahead < len(work):
                sc[i + ahead] = scores(*work[i + ahead])
            lanes = slice(hh * HEAD_SLAB, (hh + 1) * HEAD_SLAB)
            pv = jnp.dot(p.astype(BF16), v_ref[st:st + n, lanes], preferred_element_type=F32)
            z["acc"] = pv if m is None else alpha * z["acc"] + pv
            z["m"] = m_new
        for hh in range(hps):
            lanes = slice(hh * HEAD_SLAB, (hh + 1) * HEAD_SLAB)
            o_ref[:, lanes] = _da_finish(state[hh]["l"], state[hh]["acc"], tq, lam_ref[0, 0], lam_init,
                                         subln_ref[...], sg_ref[:, lanes])

    _per_q_block(pl.program_id(2), nq, body)


def _da_prompt(q, kt, v, sg, bias, table, lam, subln, batch, seq, tq, lam_init):
    t, width = q.shape
    hps = 2
    nh = width // HEAD_SLAB // hps
    nq = seq // tq
    w = hps * HEAD_SLAB
    return pl.pallas_call(
        functools.partial(_da_prompt_kernel, tq=tq, tkf=2 * tq, nq=nq, lam_init=lam_init, hps=hps),
        grid=(batch, nh, nq),
        in_specs=[
            pl.BlockSpec(memory_space=pltpu.SMEM),
            pl.BlockSpec(memory_space=pltpu.SMEM),
            pl.BlockSpec((tq, w), lambda b, h, i: (b * nq + i, h)),
            pl.BlockSpec((1, w, seq), lambda b, h, i: (b, h, 0)),
            pl.BlockSpec((seq, w), lambda b, h, i: (b, h)),
            pl.BlockSpec((tq, w), lambda b, h, i: (b * nq + i, h)),
            pl.BlockSpec((2 * hps, 2, tq, tq), lambda b, h, i: (h, 0, 0, 0)),
            pl.BlockSpec((1, HEAD_SLAB), lambda b, h, i: (0, 0)),
        ],
        out_specs=pl.BlockSpec((tq, w), lambda b, h, i: (b * nq + i, h)),
        out_shape=jax.ShapeDtypeStruct((t, width), BF16),
        compiler_params=_cparams(("parallel", "parallel", "arbitrary")),
        name="da_prompt",
    )(table, lam, q, kt, v, sg, bias, subln.reshape(1, HEAD_SLAB))


def _da_sample_kernel(lam_ref, q_ref, kct_ref, vc_ref, kn_ref, vn_ref, sg_ref, bc_ref, bn_ref,
                      subln_ref, o_ref, *, t, past, lam_init, hps):
    qpos = past + lax.broadcasted_iota(jnp.int32, (2 * t, 1), 0) % t
    vis_n = ((past + lax.broadcasted_iota(jnp.int32, (1, t), 1)) // CHUNK) <= (qpos // CHUNK)
    vis_c = (lax.broadcasted_iota(jnp.int32, (1, past), 1) // CHUNK) <= (qpos // CHUNK)
    scores = []
    for hh in range(hps):
        lanes = slice(hh * HEAD_SLAB, (hh + 1) * HEAD_SLAB)
        qq = _split_halves(q_ref[:, lanes])
        sn = lax.dot_general(qq, kn_ref[:, lanes], _NT, preferred_element_type=F32)
        sn = jnp.where(vis_n, sn + bn_ref[2 * hh:2 * hh + 2].reshape(2 * t, t), NEG_INF)
        sc = jnp.dot(qq, kct_ref[0, lanes, :].astype(BF16), preferred_element_type=F32)
        sc = jnp.where(vis_c, sc + bc_ref[2 * hh:2 * hh + 2].reshape(2 * t, past), NEG_INF)
        scores.append((sn, sc))
    for hh, (sn, sc) in enumerate(scores):
        lanes = slice(hh * HEAD_SLAB, (hh + 1) * HEAD_SLAB)
        m = jnp.maximum(jnp.max(sn, axis=-1, keepdims=True), jnp.max(sc, axis=-1, keepdims=True))
        pn = jnp.exp2(sn - m)
        pc = jnp.exp2(sc - m)
        l = jnp.sum(pn, axis=-1, keepdims=True) + jnp.sum(pc, axis=-1, keepdims=True)
        acc = (jnp.dot(pn.astype(BF16), vn_ref[:, lanes], preferred_element_type=F32)
               + jnp.dot(pc.astype(BF16), vc_ref[0, :, lanes].astype(BF16), preferred_element_type=F32))
        o_ref[:, lanes] = _da_finish(l, acc, t, lam_ref[0, 0], lam_init, subln_ref[...], sg_ref[:, lanes])


def _da_sample(q, ckt, cv, kn, vn, sg, bias_c, bias_n, lam, subln, batch, t, lam_init):
    tt, width = q.shape
    hps = 4
    nh = width // HEAD_SLAB // hps
    w = hps * HEAD_SLAB
    past = cv.shape[1]
    return pl.pallas_call(
        functools.partial(_da_sample_kernel, t=t, past=past, lam_init=lam_init, hps=hps),
        grid=(batch, nh),
        in_specs=[
            pl.BlockSpec(memory_space=pltpu.SMEM),
            pl.BlockSpec((t, w), lambda b, h: (b, h)),
            pl.BlockSpec((1, w, past), lambda b, h: (b, h, 0)),
            pl.BlockSpec((1, past, w), lambda b, h: (b, 0, h)),
            pl.BlockSpec((t, w), lambda b, h: (b, h)),
            pl.BlockSpec((t, w), lambda b, h: (b, h)),
            pl.BlockSpec((t, w), lambda b, h: (b, h)),
            pl.BlockSpec((2 * hps, t, past), lambda b, h: (h, 0, 0)),
            pl.BlockSpec((2 * hps, t, t), lambda b, h: (h, 0, 0)),
            pl.BlockSpec((1, HEAD_SLAB), lambda b, h: (0, 0)),
        ],
        out_specs=pl.BlockSpec((t, w), lambda b, h: (b, h)),
        out_shape=jax.ShapeDtypeStruct((tt, width), BF16),
        compiler_params=_cparams(("parallel", "parallel")),
        name="da_sample",
    )(lam, q, ckt, cv, kn, vn, sg, bias_c, bias_n, subln.reshape(1, HEAD_SLAB))


def _suffix_matrix(n):
    j = np.arange(n)[:, None]
    s = np.arange(n)[None, :]
    return jnp.asarray((j > s).astype(np.float32), dtype=BF16)


def _sb_logs(z, mask):
    lg = jnp.log(1.0 + jnp.exp2(-jnp.abs(z))) * LOG2E
    ls = jnp.minimum(z, 0.0) - lg
    lr = ls - z
    if mask is not None:
        lr = jnp.where(mask, lr, 0.0)
    return ls, lr


def _sb_suffix(lr, tmat):
    hi = lr.astype(BF16)
    lo = (lr - hi.astype(F32)).astype(BF16)
    return (jnp.dot(hi, tmat, preferred_element_type=F32)
            + jnp.dot(lo, tmat, preferred_element_type=F32))


def _sb_exp(ls, after, c, mask):
    arg = ls + after
    if c is not None:
        arg = arg + c
    a = jnp.exp2(arg)
    if mask is not None:
        a = jnp.where(mask, a, 0.0)
    return a.astype(BF16)


SB_DEAD_LOG2 = -160.0


def _sb_prompt_kernel(q_ref, kt_ref, vt_ref, sg_ref, tmat_ref, o_ref, acc_scr, rem_scr, *, tq, nq, hps):
    tmat = tmat_ref[...]
    r = lax.broadcasted_iota(jnp.int32, (2 * tq, tq), 0) % tq
    c = lax.broadcasted_iota(jnp.int32, (2 * tq, tq), 1)
    qqs = [_split_halves(q_ref[:, hh * HEAD_SLAB:(hh + 1) * HEAD_SLAB]) for hh in range(hps)]

    def body(qv):
        starts = [w * tq for w in range(qv, -1, -1)]

        def scores(hh, w):
            lanes = slice(hh * HEAD_SLAB, (hh + 1) * HEAD_SLAB)
            return jnp.dot(qqs[hh], kt_ref[0, lanes, starts[w]:starts[w] + tq], preferred_element_type=F32)

        def finish(pending, acc):
            ls, after, rem, mask, hh, w = pending
            lanes = slice(hh * HEAD_SLAB, (hh + 1) * HEAD_SLAB)
            a = _sb_exp(ls, after, rem, mask)
            pv = lax.dot_general(a, vt_ref[0, lanes, starts[w]:starts[w] + tq], _NT, preferred_element_type=F32)
            return pv if acc is None else acc + pv

        def run(ws, accs, rems):
            work = [(hh, w) for w in ws for hh in range(hps)]
            ahead = 2 * hps
            zs = {i: scores(*work[i]) for i in range(min(ahead, len(work)))}
            accs, rems = list(accs), list(rems)
            pending = None
            for i, (hh, w) in enumerate(work):
                mask = (c < r) if w == 0 else None
                ls, lr = _sb_logs(zs.pop(i), mask)
                if i + ahead < len(work):
                    zs[i + ahead] = scores(*work[i + ahead])
                if pending is not None:
                    accs[pending[4]] = finish(pending, accs[pending[4]])
                pending = (ls, _sb_suffix(lr, tmat), rems[hh], mask, hh, w)
                tot = jnp.sum(lr, axis=-1, keepdims=True)
                rems[hh] = tot if rems[hh] is None else rems[hh] + tot
            accs[pending[4]] = finish(pending, accs[pending[4]])
            return accs, rems

        ws = list(range(len(starts)))
        accs, rems = run(ws[:2], [None] * hps, [None] * hps)
        if len(ws) > 2:
            alive = None
            for hh in range(hps):
                acc_scr[hh] = accs[hh]
                rem_scr[hh] = rems[hh]
                mx = jnp.max(rems[hh])
                alive = mx if alive is None else jnp.maximum(alive, mx)

            @pl.when(alive > SB_DEAD_LOG2)
            def _():
                accs2, _ = run(ws[2:], [acc_scr[hh] for hh in range(hps)], [rem_scr[hh] for hh in range(hps)])
                for hh in range(hps):
                    acc_scr[hh] = accs2[hh]

            accs = [acc_scr[hh] for hh in range(hps)]
        for hh in range(hps):
            lanes = slice(hh * HEAD_SLAB, (hh + 1) * HEAD_SLAB)
            o_ref[:, lanes] = (sg_ref[:, lanes].astype(F32) * _join_halves(accs[hh], tq)).astype(BF16)

    _per_q_block(pl.program_id(2), nq, body)


def _sb_prompt(q, kt, vt, sg, batch, seq, tq):
    t, width = q.shape
    hps = 2
    nh = width // HEAD_SLAB // hps
    w = hps * HEAD_SLAB
    nq = seq // tq
    return pl.pallas_call(
        functools.partial(_sb_prompt_kernel, tq=tq, nq=nq, hps=hps),
        grid=(batch, nh, nq),
        in_specs=[
            pl.BlockSpec((tq, w), lambda b, h, i: (b * nq + i, h)),
            pl.BlockSpec((1, w, seq), lambda b, h, i: (b, h, 0)),
            pl.BlockSpec((1, w, seq), lambda b, h, i: (b, h, 0)),
            pl.BlockSpec((tq, w), lambda b, h, i: (b * nq + i, h)),
            pl.BlockSpec((tq, tq), lambda b, h, i: (0, 0)),
        ],
        out_specs=pl.BlockSpec((tq, w), lambda b, h, i: (b * nq + i, h)),
        out_shape=jax.ShapeDtypeStruct((t, width), BF16),
        scratch_shapes=[pltpu.VMEM((hps, 2 * tq, HEAD_SLAB), F32), pltpu.VMEM((hps, 2 * tq, 1), F32)],
        compiler_params=_cparams(("parallel", "parallel", "arbitrary")),
        name="sb_prompt",
    )(q, kt, vt, sg, _suffix_matrix(tq))


def _sb_sample_kernel(q_ref, kct_ref, vct_ref, kn_ref, vn_ref, sg_ref, tmat_ref, o_ref, *, t, tn, tk, past, hps):
    tmat = tmat_ref[...]
    pad = jnp.zeros((tn - t, HEAD_SLAB), BF16)
    r = lax.broadcasted_iota(jnp.int32, (2 * t, tn), 0) % t
    c = lax.broadcasted_iota(jnp.int32, (2 * t, tn), 1)
    blocks = []
    for hh in range(hps):
        lanes = slice(hh * HEAD_SLAB, (hh + 1) * HEAD_SLAB)
        qq = _split_halves(q_ref[:, lanes])
        kn = jnp.concatenate([kn_ref[:, lanes], pad], axis=0)
        z = lax.dot_general(qq, kn, _NT, preferred_element_type=F32)
        blocks.append(dict(hh=hh, z=z, mask=c < r, tm=tmat[:tn, :tn], j=None))
        for j in reversed(range(past // tk)):
            z = jnp.dot(qq, kct_ref[0, lanes, j * tk:(j + 1) * tk].astype(BF16), preferred_element_type=F32)
            blocks.append(dict(hh=hh, z=z, mask=None, tm=tmat, j=j))
    rem = [None] * hps
    for bk in blocks:
        bk["ls"], bk["lr"] = _sb_logs(bk.pop("z"), bk["mask"])
        bk["rem"] = rem[bk["hh"]]
        tot = jnp.sum(bk["lr"], axis=-1, keepdims=True)
        rem[bk["hh"]] = tot if bk["rem"] is None else bk["rem"] + tot
    for bk in blocks:
        bk["after"] = _sb_suffix(bk.pop("lr"), bk["tm"])
    acc = [None] * hps
    for bk in blocks:
        hh, j = bk["hh"], bk["j"]
        lanes = slice(hh * HEAD_SLAB, (hh + 1) * HEAD_SLAB)
        a = _sb_exp(bk["ls"], bk["after"], bk["rem"], bk["mask"])
        if j is None:
            vn = jnp.concatenate([vn_ref[:, lanes], pad], axis=0)
            pv = jnp.dot(a, vn, preferred_element_type=F32)
        else:
            vj = vct_ref[0, lanes, j * tk:(j + 1) * tk].astype(BF16)
            pv = lax.dot_general(a, vj, _NT, preferred_element_type=F32)
        acc[hh] = pv if acc[hh] is None else acc[hh] + pv
    for hh in range(hps):
        lanes = slice(hh * HEAD_SLAB, (hh + 1) * HEAD_SLAB)
        o_ref[:, lanes] = (sg_ref[:, lanes].astype(F32) * _join_halves(acc[hh], t)).astype(BF16)


def _sb_sample(q, ckt, cvt, kn, vn, sg, batch, t, tk):
    tt, width = q.shape
    hps = 4
    nh = width // HEAD_SLAB // hps
    w = hps * HEAD_SLAB
    past = ckt.shape[2]
    tk = min(tk, past)
    assert past % tk == 0
    tn = HEAD_SLAB
    assert t <= tn <= tk
    return pl.pallas_call(
        functools.partial(_sb_sample_kernel, t=t, tn=tn, tk=tk, past=past, hps=hps),
        grid=(batch, nh),
        in_specs=[
            pl.BlockSpec((t, w), lambda b, h: (b, h)),
            pl.BlockSpec((1, w, past), lambda b, h: (b, h, 0)),
            pl.BlockSpec((1, w, past), lambda b, h: (b, h, 0)),
            pl.BlockSpec((t, w), lambda b, h: (b, h)),
            pl.BlockSpec((t, w), lambda b, h: (b, h)),
            pl.BlockSpec((t, w), lambda b, h: (b, h)),
            pl.BlockSpec((tk, tk), lambda b, h: (0, 0)),
        ],
        out_specs=pl.BlockSpec((t, w), lambda b, h: (b, h)),
        out_shape=jax.ShapeDtypeStruct((tt, width), BF16),
        compiler_params=_cparams(("parallel", "parallel")),
        name="sb_sample",
    )(q, ckt, cvt, kn, vn, sg, _suffix_matrix(tk))


def _sw_group(sink_ref, q_slabs, kk, vv, bias, valid, kvh, group, t):
    qst = jnp.concatenate([_split_halves(qs) for qs in q_slabs], axis=0)
    s = lax.dot_general(qst, kk, _NT, preferred_element_type=F32) + bias
    if valid is not None:
        s = jnp.where(valid, s, NEG_INF)
    row = lax.broadcasted_iota(jnp.int32, (group * t, 1), 0)
    sk = jnp.zeros((group * t, 1), F32)
    for g in range(group):
        sk = jnp.where(row // t == g, sink_ref[kvh * group + g] * LOG2E, sk)
    m = jnp.maximum(jnp.max(s, axis=-1, keepdims=True), sk)
    p = jnp.exp2(s - m)
    w = p / (jnp.sum(p, axis=-1, keepdims=True) + jnp.exp2(sk - m))
    o = jnp.dot(w.astype(BF16), vv, preferred_element_type=F32)
    return [_join_halves(o[2 * i * t:(2 * i + 2) * t], t) for i in range(group // 2)]


def _sw_prompt_kernel(sink_ref, q_ref, kp_ref, kc_ref, vp_ref, vc_ref, sg_ref, bias_ref, o_ref,
                      *, tq, n_kv, group):
    band = (WIN_CHUNKS + 1) * CHUNK
    kcat = jnp.concatenate([kp_ref[...], kc_ref[...]], axis=0)
    vcat = jnp.concatenate([vp_ref[...], vc_ref[...]], axis=0)
    col = lax.broadcasted_iota(jnp.int32, (1, band), 1)

    def body(first):
        for cc in range(tq // CHUNK):
            r0 = cc * CHUNK
            valid = (col >= WINDOW - r0) if first and r0 < WINDOW else None
            for kvh in range(n_kv):
                kk = kcat[r0:r0 + band, kvh * HEAD_SLAB:(kvh + 1) * HEAD_SLAB]
                vv = vcat[r0:r0 + band, kvh * HEAD_SLAB:(kvh + 1) * HEAD_SLAB]
                s0 = kvh * (group // 2)
                q_slabs = [q_ref[r0:r0 + CHUNK, (s0 + i) * HEAD_SLAB:(s0 + i + 1) * HEAD_SLAB]
                           for i in range(group // 2)]
                bias = bias_ref[kvh * group:(kvh + 1) * group].reshape(group * CHUNK, band)
                outs = _sw_group(sink_ref, q_slabs, kk, vv, bias, valid, kvh, group, CHUNK)
                for i, o in enumerate(outs):
                    lanes = slice((s0 + i) * HEAD_SLAB, (s0 + i + 1) * HEAD_SLAB)
                    sg = sg_ref[r0:r0 + CHUNK, lanes].astype(F32)
                    o_ref[r0:r0 + CHUNK, lanes] = (sg * o).astype(BF16)

    qi = pl.program_id(1)
    pl.when(qi == 0)(functools.partial(body, True))
    pl.when(qi > 0)(functools.partial(body, False))


def _sw_prompt(q, kd, vd, sg, bias, sinks, batch, seq, tq, n_kv, group):
    t, width = q.shape
    nq = seq // tq
    wpb = tq // WINDOW
    kvw = kd.shape[1]
    band = (WIN_CHUNKS + 1) * CHUNK
    prev = lambda b, i: (jnp.maximum((b * nq + i) * wpb - 1, 0), 0)
    cur = lambda b, i: (b * nq + i, 0)
    return pl.pallas_call(
        functools.partial(_sw_prompt_kernel, tq=tq, n_kv=n_kv, group=group),
        grid=(batch, nq),
        in_specs=[
            pl.BlockSpec(memory_space=pltpu.SMEM),
            pl.BlockSpec((tq, width), cur),
            pl.BlockSpec((WINDOW, kvw), prev),
            pl.BlockSpec((tq, kvw), cur),
            pl.BlockSpec((WINDOW, kvw), prev),
            pl.BlockSpec((tq, kvw), cur),
            pl.BlockSpec((tq, width), cur),
            pl.BlockSpec((n_kv * group, CHUNK, band), lambda b, i: (0, 0, 0)),
        ],
        out_specs=pl.BlockSpec((tq, width), cur),
        out_shape=jax.ShapeDtypeStruct((t, width), BF16),
        compiler_params=_cparams(("parallel", "arbitrary")),
        name="sw_prompt",
    )(sinks, q, kd, kd, vd, vd, sg, bias)


def _sw_sample_kernel(sink_ref, q_ref, k_ref, v_ref, sg_ref, bias_ref, o_ref, *, t, past, wb, n_kv, group):
    band = wb + t
    qpos = past + lax.broadcasted_iota(jnp.int32, (group * t, 1), 0) % t
    kpos = past - wb + lax.broadcasted_iota(jnp.int32, (1, band), 1)
    qc = qpos // CHUNK
    kc = kpos // CHUNK
    valid = (kc <= qc) & (kc >= qc - WIN_CHUNKS)
    for kvh in range(n_kv):
        kk = k_ref[0, :, kvh * HEAD_SLAB:(kvh + 1) * HEAD_SLAB]
        vv = v_ref[0, :, kvh * HEAD_SLAB:(kvh + 1) * HEAD_SLAB]
        s0 = kvh * (group // 2)
        q_slabs = [q_ref[:, (s0 + i) * HEAD_SLAB:(s0 + i + 1) * HEAD_SLAB] for i in range(group // 2)]
        bias = bias_ref[kvh * group:(kvh + 1) * group].reshape(group * t, band)
        outs = _sw_group(sink_ref, q_slabs, kk, vv, bias, valid, kvh, group, t)
        for i, o in enumerate(outs):
            lanes = slice((s0 + i) * HEAD_SLAB, (s0 + i + 1) * HEAD_SLAB)
            o_ref[:, lanes] = (sg_ref[:, lanes].astype(F32) * o).astype(BF16)


def _sw_sample(q, kd, vd, sg, bias, sinks, batch, t, past, wb, n_kv, group):
    tt, width = q.shape
    band, kvw = kd.shape[1:]
    return pl.pallas_call(
        functools.partial(_sw_sample_kernel, t=t, past=past, wb=wb, n_kv=n_kv, group=group),
        grid=(batch,),
        in_specs=[
            pl.BlockSpec(memory_space=pltpu.SMEM),
            pl.BlockSpec((t, width), lambda b: (b, 0)),
            pl.BlockSpec((1, band, kvw), lambda b: (b, 0, 0)),
            pl.BlockSpec((1, band, kvw), lambda b: (b, 0, 0)),
            pl.BlockSpec((t, width), lambda b: (b, 0)),
            pl.BlockSpec((n_kv * group, t, band), lambda b: (0, 0, 0)),
        ],
        out_specs=pl.BlockSpec((t, width), lambda b: (b, 0)),
        out_shape=jax.ShapeDtypeStruct((tt, width), BF16),
        compiler_params=_cparams(("parallel",)),
        name="sw_sample",
    )(sinks, q, kd, vd, sg, bias)


TM_PROJ = 512
TQ_ATTN = 256


def _feature_major(a):
    nd = a.ndim
    a = jnp.transpose(a, (0,) + tuple(range(2, nd)) + (1,))
    return a.reshape(a.shape[0], -1, a.shape[-1])


def _position_major(a, feat_shape):
    b, _, p = a.shape
    a = a.reshape((b,) + tuple(feat_shape) + (p,))
    nd = a.ndim
    return jnp.transpose(a, (0, nd - 1) + tuple(range(1, nd - 1)))


def _project(x, res, norm, w_nat, plan, dtypes, tm, **kw):
    outs = _inproj(x, norm, w_nat, plan, dtypes, tm, res=res, **kw)
    return (x, outs) if res is None else (outs[0], outs[1:])


def _da_layer(xp, xs, res_p, res_s, ck, cv, table, norm, w_in, lam_params, subln, lam_init, dims):
    batch, seq, dec_batch, t_dec, past, d = dims
    br = d
    wq, wk, wv, wg = (w_in[:, i * br:(i + 1) * br] for i in range(4))
    w_p = jnp.concatenate([wq, wv, wg], axis=1).astype(BF16)
    plan_p = [(0, br, "qscale2"), (br, br, "none"), (2 * br, br, "silu"), (br, br, "none")]
    xp, (qp, vp, sgp, vpf, ktf, kt) = _project(
        xp, res_p, norm, w_p, plan_p, [BF16, BF16, BF16, F32], TM_PROJ,
        w_t=wk.T.astype(BF16), plan_t=[(0, br), (0, br)], dtypes_t=[F32, BF16], seq=seq)
    plan_s = [(0, br, "qscale2"), (br, br, "none"), (2 * br, br, "none"), (3 * br, br, "silu"),
              (br, br, "none"), (2 * br, br, "none")]
    xs, (qs, ks, vs, sgs, ksf, vsf) = _project(xs, res_s, norm, w_in.astype(BF16), plan_s,
                                               [BF16, BF16, BF16, BF16, F32, F32], TM_PROJ)
    lam = _diff_lambda(lam_params, lam_init)
    tq = min(TQ_ATTN, seq)
    assert seq % tq == 0 and tq >= MAX_DISTANCE and tq % CHUNK == 0
    assert past == ck.shape[1] and past % CHUNK == 0 and t_dec <= CHUNK
    i = jnp.arange(tq)[:, None]
    j = jnp.arange(tq)[None, :]
    rel_p = jnp.concatenate([j - i, j - tq - i], axis=0)
    nbh = table.shape[1]
    bias_p = _bias_tiles(table, rel_p, LOG2E).reshape(nbh, 2, tq, tq)
    og_p = _da_prompt(qp, kt, vp, sgp, bias_p, table, lam, subln, batch, seq, tq, lam_init)
    q_pos = past + jnp.arange(t_dec)[:, None]
    bias_c = _bias_tiles(table, jnp.arange(past)[None, :] - q_pos, LOG2E)
    bias_n = _bias_tiles(table, past + jnp.arange(t_dec)[None, :] - q_pos, LOG2E)
    og_s = _da_sample(qs, _feature_major(ck), cv.reshape(dec_batch, past, br), ks, vs, sgs,
                      bias_c, bias_n, lam, subln, dec_batch, t_dec, lam_init)
    state = (_position_major(ktf, ck.shape[2:]), vpf.reshape((batch, seq) + cv.shape[2:]),
             ksf.reshape((dec_batch, t_dec) + ck.shape[2:]), vsf.reshape((dec_batch, t_dec) + cv.shape[2:]))
    return xp, xs, og_p, og_s, state


def _sb_layer(xp, xs, res_p, res_s, ck, cv, norm, w_in, dims):
    batch, seq, dec_batch, t_dec, past, d = dims
    br = d
    wq, wk, wv, wg = (w_in[:, i * br:(i + 1) * br] for i in range(4))
    w_p = jnp.concatenate([wq, wg], axis=1).astype(BF16)
    w_t = jnp.concatenate([wk, wv], axis=1).T.astype(BF16)
    xp, (qp, sgp, ktf, kt, vtf, vt) = _project(
        xp, res_p, norm, w_p, [(0, br, "qscale2"), (br, br, "silu")], [BF16, BF16], TM_PROJ,
        w_t=w_t, plan_t=[(0, br), (0, br), (br, br), (br, br)], dtypes_t=[F32, BF16, F32, BF16], seq=seq)
    plan_s = [(0, br, "qscale2"), (br, br, "none"), (2 * br, br, "none"), (3 * br, br, "silu"),
              (br, br, "none"), (2 * br, br, "none")]
    xs, (qs, ks, vs, sgs, ksf, vsf) = _project(xs, res_s, norm, w_in.astype(BF16), plan_s,
                                               [BF16, BF16, BF16, BF16, F32, F32], TM_PROJ)
    tq = min(TQ_ATTN, seq)
    assert seq % tq == 0 and past == ck.shape[1]
    og_p = _sb_prompt(qp, kt, vt, sgp, batch, seq, tq)
    og_s = _sb_sample(qs, _feature_major(ck), _feature_major(cv), ks, vs, sgs, dec_batch, t_dec, TQ_ATTN)
    state = (_position_major(ktf, ck.shape[2:]), _position_major(vtf, cv.shape[2:]),
             ksf.reshape((dec_batch, t_dec) + ck.shape[2:]), vsf.reshape((dec_batch, t_dec) + cv.shape[2:]))
    return xp, xs, og_p, og_s, state


def _dup_heads(a, n_kv, hd):
    lead = a.shape[:-1]
    a = a.reshape(lead + (n_kv, 1, hd))
    return jnp.broadcast_to(a, lead + (n_kv, 2, hd)).reshape(lead + (n_kv * 2 * hd,))


def _sw_layer(xp, xs, res_p, res_s, ck, cv, table, norm, w_in, sinks, dims):
    batch, seq, dec_batch, t_dec, past, d = dims
    br = d
    wb, n_kv, hd = ck.shape[1:]
    assert hd == HALF
    kvw = n_kv * hd
    group = br // hd // n_kv
    assert group % 2 == 0
    wq, wk, wv, wg = (w_in[:, :br], w_in[:, br:br + kvw], w_in[:, br + kvw:br + 2 * kvw],
                      w_in[:, br + 2 * kvw:])
    w_p = jnp.concatenate([wq, _dup_heads(wk, n_kv, hd), _dup_heads(wv, n_kv, hd), wg], axis=1).astype(BF16)
    plan_p = [(0, br, "qscale2"), (br, 2 * kvw, "none"), (br + 2 * kvw, 2 * kvw, "none"),
              (br + 4 * kvw, br, "silu")]
    xp, (qp, kdp, vdp, sgp) = _project(xp, res_p, norm, w_p, plan_p, [BF16] * 4, TM_PROJ)
    keep = min(WINDOW, seq)
    w_kv_t = jnp.concatenate([wk, wv], axis=1).T.astype(BF16)
    x_tail = xp.reshape(batch, seq, d)[:, seq - keep:].reshape(batch * keep, d)
    ktf, vtf = _inproj(x_tail, norm, None, [], [], keep, w_t=w_kv_t,
                       plan_t=[(0, kvw), (kvw, kvw)], dtypes_t=[F32, F32], seq=keep)
    plan_s = [(0, br, "qscale2"), (br, kvw, "none"), (br + kvw, kvw, "none"), (br + 2 * kvw, br, "silu")]
    xs, (qs, ksf, vsf, sgs) = _project(xs, res_s, norm, w_in.astype(BF16), plan_s, [BF16, F32, F32, BF16],
                                       TM_PROJ)

    tq = min(TQ_ATTN, seq)
    assert seq % tq == 0 and tq % WINDOW == 0
    band = (WIN_CHUNKS + 1) * CHUNK
    rel_p = (jnp.arange(band)[None, :] - WIN_CHUNKS * CHUNK) - jnp.arange(CHUNK)[:, None]
    bias_p = _bias_tiles(table, rel_p, LOG2E)
    og_p = _sw_prompt(qp, kdp, vdp, sgp, bias_p, sinks, batch, seq, tq, n_kv, group)

    k_all = jnp.concatenate([ck, ksf.reshape(dec_batch, t_dec, n_kv, hd)], axis=1)
    v_all = jnp.concatenate([cv, vsf.reshape(dec_batch, t_dec, n_kv, hd)], axis=1)
    kd_s = _dup_heads(k_all.reshape(dec_batch, wb + t_dec, kvw), n_kv, hd).astype(BF16)
    vd_s = _dup_heads(v_all.reshape(dec_batch, wb + t_dec, kvw), n_kv, hd).astype(BF16)
    rel_s = (past - wb + jnp.arange(wb + t_dec)[None, :]) - (past + jnp.arange(t_dec)[:, None])
    bias_s = _bias_tiles(table, rel_s, LOG2E)
    og_s = _sw_sample(qs, kd_s, vd_s, sgs, bias_s, sinks, dec_batch, t_dec, past, wb, n_kv, group)
    state = (_position_major(ktf, (n_kv, hd)), _position_major(vtf, (n_kv, hd)),
             k_all[:, -wb:], v_all[:, -wb:])
    return xp, xs, og_p, og_s, state


def kernel(x_prompt, x_sample, cache_k_0, cache_v_0, cache_k_1, cache_v_1, cache_k_2, cache_v_2,
           cache_k_3, cache_v_3, rel_bias_table,
           norm_0, w_in_0, w_out_0, da_lambda_0, da_subln_0,
           norm_1, w_in_1, w_out_1,
           norm_2, w_in_2, w_out_2, sw_sinks_2,
           norm_3, w_in_3, w_out_3, da_lambda_3, da_subln_3,
           final_norm):
    batch, seq, d = x_prompt.shape
    dec_batch, t_dec, _ = x_sample.shape
    dims = (batch, seq, dec_batch, t_dec, cache_k_0.shape[1], d)
    xp = x_prompt.reshape(batch * seq, d)
    xs = x_sample.reshape(dec_batch * t_dec, d)
    layers = [
        ("da", cache_k_0, cache_v_0, norm_0, w_in_0, w_out_0, (da_lambda_0, da_subln_0)),
        ("sb", cache_k_1, cache_v_1, norm_1, w_in_1, w_out_1, ()),
        ("sw", cache_k_2, cache_v_2, norm_2, w_in_2, w_out_2, (sw_sinks_2,)),
        ("da", cache_k_3, cache_v_3, norm_3, w_in_3, w_out_3, (da_lambda_3, da_subln_3)),
    ]
    states = []
    res_p = res_s = None
    for i, (kind, ck, cv, norm, w_in, w_out, extra) in enumerate(layers):
        if kind == "da":
            lam_init = 0.8 - 0.6 * math.exp(-0.3 * i)
            xp, xs, og_p, og_s, st = _da_layer(xp, xs, res_p, res_s, ck, cv, rel_bias_table, norm, w_in,
                                               extra[0], extra[1], lam_init, dims)
        elif kind == "sb":
            xp, xs, og_p, og_s, st = _sb_layer(xp, xs, res_p, res_s, ck, cv, norm, w_in, dims)
        else:
            xp, xs, og_p, og_s, st = _sw_layer(xp, xs, res_p, res_s, ck, cv, rel_bias_table, norm, w_in,
                                               extra[0], dims)
        w_o = w_out.astype(BF16)
        res_p, res_s = (og_p, w_o), (og_s, w_o)
        states.append(st)
    xp = _outproj(xp, res_p[0], res_p[1], final_norm, TM_PROJ)
    xs = _outproj(xs, res_s[0], res_s[1], final_norm, TM_PROJ)
    out = [xp.reshape(batch, seq, d), xs.reshape(dec_batch, t_dec, d)]
    for st in states:
        out.extend(st)
    return tuple(out)
```

```python
import functools
import math

import jax
import jax.numpy as jnp
import numpy as np
from jax import lax
from jax.experimental import pallas as pl
from jax.experimental.pallas import tpu as pltpu

F32 = jnp.float32
BF16 = jnp.bfloat16

EPS = 1e-6
NEG_INF = -1e30
LOG2E = math.log2(math.e)
CHUNK = 64
WINDOW = 128
WIN_CHUNKS = WINDOW // CHUNK
N_BUCKETS = 32
MAX_DISTANCE = 128
FAR_BUCKET = N_BUCKETS // 2 - 1
HEAD_SLAB = 128
HALF = HEAD_SLAB // 2
VMEM_LIMIT = 48 * 1024 * 1024

_NT = (((1,), (1,)), ((), ()))


def _cparams(sem):
    return pltpu.CompilerParams(dimension_semantics=sem, vmem_limit_bytes=VMEM_LIMIT)


def _inproj_kernel(*refs, plan, plan_t, has_nat, has_res):
    x_ref, refs = refs[0], refs[1:]
    x = x_ref[...]
    if has_res:
        (og_ref, wo_ref), refs = refs[:2], refs[2:]
        x = x + jnp.dot(og_ref[...], wo_ref[...], preferred_element_type=F32)
    g_ref, refs = refs[0], refs[1:]
    if has_nat:
        w_ref, refs = refs[0], refs[1:]
    if plan_t:
        wt_ref, refs = refs[0], refs[1:]
    if has_res:
        refs[0][...] = x
        refs = refs[1:]
    y = x * lax.rsqrt(jnp.mean(x * x, axis=-1, keepdims=True) + EPS)
    xn = (y * g_ref[...]).astype(BF16)
    if has_nat:
        h = jnp.dot(xn, w_ref[...], preferred_element_type=F32)
        for o_ref, (c0, width, kind) in zip(refs, plan):
            t = h[:, c0:c0 + width]
            if kind == "qscale2":
                t = t * (HALF ** -0.5 * LOG2E)
            elif kind == "silu":
                t = t * jax.nn.sigmoid(t)
            o_ref[...] = t.astype(o_ref.dtype)
        refs = refs[len(plan):]
    if plan_t:
        ht = lax.dot_general(wt_ref[...], xn, _NT, preferred_element_type=F32)
        for o_ref, (r0, nrows) in zip(refs, plan_t):
            o_ref[0] = ht[r0:r0 + nrows].astype(o_ref.dtype)


def _inproj(x2d, g, w_nat, plan, dtypes, tm, w_t=None, plan_t=(), dtypes_t=(), seq=None, res=None):
    t, d = x2d.shape
    tm = min(tm, t)
    assert t % tm == 0
    in_specs = [pl.BlockSpec((tm, d), lambda i: (i, 0))]
    args = [x2d]
    out_shape, out_specs = [], []
    if res is not None:
        og, w_out = res
        in_specs += [pl.BlockSpec((tm, og.shape[1]), lambda i: (i, 0)), pl.BlockSpec(w_out.shape, lambda i: (0, 0))]
        args += [og, w_out]
        out_shape.append(jax.ShapeDtypeStruct((t, d), F32))
        out_specs.append(pl.BlockSpec((tm, d), lambda i: (i, 0)))
    in_specs.append(pl.BlockSpec((1, d), lambda i: (0, 0)))
    args.append(g.reshape(1, d))
    if w_nat is not None:
        in_specs.append(pl.BlockSpec(w_nat.shape, lambda i: (0, 0)))
        args.append(w_nat)
        out_shape += [jax.ShapeDtypeStruct((t, width), dt) for (_, width, _), dt in zip(plan, dtypes)]
        out_specs += [pl.BlockSpec((tm, width), lambda i: (i, 0)) for (_, width, _) in plan]
    if plan_t:
        assert seq % tm == 0 and t % seq == 0
        tps = seq // tm
        in_specs.append(pl.BlockSpec(w_t.shape, lambda i: (0, 0)))
        args.append(w_t)
        out_shape += [jax.ShapeDtypeStruct((t // seq, nrows, seq), dt) for (_, nrows), dt in zip(plan_t, dtypes_t)]
        out_specs += [pl.BlockSpec((1, nrows, tm), lambda i: (i // tps, 0, i % tps)) for (_, nrows) in plan_t]
    return pl.pallas_call(
        functools.partial(_inproj_kernel, plan=tuple(plan), plan_t=tuple(plan_t), has_nat=w_nat is not None,
                          has_res=res is not None),
        grid=(t // tm,),
        in_specs=in_specs,
        out_specs=out_specs,
        out_shape=out_shape,
        compiler_params=_cparams(("parallel",)),
        name="inproj",
    )(*args)


def _outproj_kernel(x_ref, og_ref, w_ref, g_ref, o_ref):
    x = x_ref[...] + jnp.dot(og_ref[...], w_ref[...], preferred_element_type=F32)
    y = x * lax.rsqrt(jnp.mean(x * x, axis=-1, keepdims=True) + EPS)
    o_ref[...] = y * g_ref[...]


def _outproj(x2d, og, w_bf16, final_g, tm):
    t, d = x2d.shape
    br = og.shape[1]
    tm = min(tm, t)
    assert t % tm == 0
    return pl.pallas_call(
        _outproj_kernel,
        grid=(t // tm,),
        in_specs=[
            pl.BlockSpec((tm, d), lambda i: (i, 0)),
            pl.BlockSpec((tm, br), lambda i: (i, 0)),
            pl.BlockSpec((br, d), lambda i: (0, 0)),
            pl.BlockSpec((1, d), lambda i: (0, 0)),
        ],
        out_specs=pl.BlockSpec((tm, d), lambda i: (i, 0)),
        out_shape=jax.ShapeDtypeStruct((t, d), F32),
        compiler_params=_cparams(("parallel",)),
        name="outproj",
    )(x2d, og, w_bf16, final_g.reshape(1, d))


def _rel_bucket(rel):
    nb = N_BUCKETS // 2
    max_exact = nb // 2
    n = jnp.abs(rel)
    nf = jnp.maximum(n, 1).astype(F32)
    large = max_exact + (jnp.log(nf / max_exact) / math.log(MAX_DISTANCE / max_exact)
                         * (nb - max_exact)).astype(jnp.int32)
    large = jnp.minimum(large, nb - 1)
    return jnp.where(rel > 0, nb, 0) + jnp.where(n < max_exact, n, large)


def _bias_kernel(tab_ref, idx_ref, o_ref, *, scale):
    h = pl.program_id(0)
    idx = idx_ref[...]
    acc = jnp.zeros(idx.shape, F32)
    for b in range(N_BUCKETS):
        acc = jnp.where(idx == b, tab_ref[b, h], acc)
    o_ref[0] = acc * scale if scale != 1.0 else acc


def _bias_tiles(table, rel, scale=1.0):
    idx = _rel_bucket(rel).astype(jnp.int32)
    r, c = idx.shape
    nh = table.shape[1]
    return pl.pallas_call(
        functools.partial(_bias_kernel, scale=scale),
        grid=(nh,),
        in_specs=[
            pl.BlockSpec(memory_space=pltpu.SMEM),
            pl.BlockSpec((r, c), lambda h: (0, 0)),
        ],
        out_specs=pl.BlockSpec((1, r, c), lambda h: (h, 0, 0)),
        out_shape=jax.ShapeDtypeStruct((nh, r, c), F32),
        compiler_params=_cparams(("arbitrary",)),
        name="bias_tiles",
    )(table, idx)


def _split_halves(q_bf16):
    qf = q_bf16.astype(F32)
    lane = lax.broadcasted_iota(jnp.int32, qf.shape, 1)
    qa = jnp.where(lane < HALF, qf, 0.0).astype(BF16)
    qb = jnp.where(lane >= HALF, qf, 0.0).astype(BF16)
    return jnp.concatenate([qa, qb], axis=0)


def _join_halves(o, t):
    lane = lax.broadcasted_iota(jnp.int32, (t, HEAD_SLAB), 1)
    return jnp.where(lane < HALF, o[:t], o[t:])


def _da_finish(l, acc, t, lam, lam_init, subln, sg):
    o = acc[:t] / l[:t] - lam * (acc[t:] / l[t:])
    o = o * lax.rsqrt(jnp.mean(o * o, axis=-1, keepdims=True) + EPS) * subln
    o = o * (1.0 - lam_init)
    return (sg.astype(F32) * o).astype(BF16)


def _lam_kernel(lp_ref, o_ref, *, lam_init):
    lp = lp_ref[...]
    a = jnp.sum(lp[0:1] * lp[1:2], axis=-1, keepdims=True)
    b = jnp.sum(lp[2:3] * lp[3:4], axis=-1, keepdims=True)
    o_ref[...] = jnp.exp(a) - jnp.exp(b) + lam_init


def _diff_lambda(lam_params, lam_init):
    return pl.pallas_call(
        functools.partial(_lam_kernel, lam_init=lam_init),
        out_shape=jax.ShapeDtypeStruct((1, 1), F32),
        name="diff_lambda",
    )(lam_params)


def _per_q_block(qi, nq, body):
    for qv in range(nq):
        pl.when(qi == qv)(functools.partial(body, qv))


def _da_prompt_kernel(tab_ref, lam_ref, q_ref, kt_ref, v_ref, sg_ref, bias_ref, subln_ref, o_ref,
                      *, tq, tkf, nq, lam_init, hps):
    hg = pl.program_id(1)
    rows = 2 * tq
    r = lax.broadcasted_iota(jnp.int32, (rows, tq), 0) % tq
    c = lax.broadcasted_iota(jnp.int32, (rows, tq), 1)
    row = lax.broadcasted_iota(jnp.int32, (rows, 1), 0)
    qqs, fars = [], []
    for hh in range(hps):
        h = hg * hps + hh
        qqs.append(_split_halves(q_ref[:, hh * HEAD_SLAB:(hh + 1) * HEAD_SLAB]))
        fars.append(LOG2E * jnp.where(row < tq, tab_ref[FAR_BUCKET, 2 * h], tab_ref[FAR_BUCKET, 2 * h + 1]))

    def body(qv):
        items = [(qv * tq, tq, "diag")]
        if qv >= 1:
            items.append(((qv - 1) * tq, tq, "prev"))
        nfar = max(qv - 1, 0) * tq
        for pos in range(0, nfar, tkf):
            items.append((pos, min(tkf, nfar - pos), "far"))
        work = [(hh, w) for w in range(len(items)) for hh in range(hps)]

        def scores(hh, w):
            st, n, kind = items[w]
            lanes = slice(hh * HEAD_SLAB, (hh + 1) * HEAD_SLAB)
            s = jnp.dot(qqs[hh], kt_ref[0, lanes, st:st + n], preferred_element_type=F32)
            if kind == "diag":
                s = s + bias_ref[2 * hh:2 * hh + 2, 0].reshape(rows, tq)
                s = jnp.where((c // CHUNK) <= (r // CHUNK), s, NEG_INF)
            elif kind == "prev":
                s = s + bias_ref[2 * hh:2 * hh + 2, 1].reshape(rows, tq)
            return s

        ahead = 2 * hps
        sc = {i: scores(*work[i]) for i in range(min(ahead, len(work)))}
        state = [dict(m=None, acc=None, far=False) for _ in range(hps)]
        for i, (hh, w) in enumerate(work):
            st, n, kind = items[w]
            z = state[hh]
            s = sc.pop(i)
            m = z["m"]
            if kind == "far" and not z["far"]:
                m = m - fars[hh]
                z["far"] = True
            bm = jnp.max(s, axis=-1, keepdims=True)
            if m is None:
                m_new = bm
            else:
                m_new = jnp.maximum(m, bm)
                alpha = jnp.exp2(m - m_new)
            p = jnp.exp2((s - m_new).astype(BF16))
            if i + ahead < len(work):
                sc[i + ahead] = scores(*work[i + ahead])
            lanes = slice(hh * HEAD_SLAB, (hh + 1) * HEAD_SLAB)
            vx = jnp.concatenate([v_ref[st:st + n, lanes], jnp.ones((n, HEAD_SLAB), BF16)], axis=1)
            pv = jnp.dot(p, vx, preferred_element_type=F32)
            z["acc"] = pv if m is None else alpha * z["acc"] + pv
            z["m"] = m_new
        for hh in range(hps):
            lanes = slice(hh * HEAD_SLAB, (hh + 1) * HEAD_SLAB)
            acc = state[hh]["acc"]
            o_ref[:, lanes] = _da_finish(acc[:, HEAD_SLAB:], acc[:, :HEAD_SLAB], tq, lam_ref[0, 0], lam_init,
                                         subln_ref[...], sg_ref[:, lanes])

    _per_q_block(pl.program_id(2), nq, body)


def _da_prompt(q, kt, v, sg, bias, table, lam, subln, batch, seq, tq, lam_init):
    t, width = q.shape
    hps = 2
    nh = width // HEAD_SLAB // hps
    nq = seq // tq
    w = hps * HEAD_SLAB
    return pl.pallas_call(
        functools.partial(_da_prompt_kernel, tq=tq, tkf=2 * tq, nq=nq, lam_init=lam_init, hps=hps),
        grid=(batch, nh, nq),
        in_specs=[
            pl.BlockSpec(memory_space=pltpu.SMEM),
            pl.BlockSpec(memory_space=pltpu.SMEM),
            pl.BlockSpec((tq, w), lambda b, h, i: (b * nq + i, h)),
            pl.BlockSpec((1, w, seq), lambda b, h, i: (b, h, 0)),
            pl.BlockSpec((seq, w), lambda b, h, i: (b, h)),
            pl.BlockSpec((tq, w), lambda b, h, i: (b * nq + i, h)),
            pl.BlockSpec((2 * hps, 2, tq, tq), lambda b, h, i: (h, 0, 0, 0)),
            pl.BlockSpec((1, HEAD_SLAB), lambda b, h, i: (0, 0)),
        ],
        out_specs=pl.BlockSpec((tq, w), lambda b, h, i: (b * nq + i, h)),
        out_shape=jax.ShapeDtypeStruct((t, width), BF16),
        compiler_params=_cparams(("parallel", "parallel", "arbitrary")),
        name="da_prompt",
    )(table, lam, q, kt, v, sg, bias, subln.reshape(1, HEAD_SLAB))


def _da_sample_kernel(lam_ref, q_ref, kct_ref, vc_ref, kn_ref, vn_ref, sg_ref, bc_ref, bn_ref,
                      subln_ref, o_ref, *, t, past, lam_init, hps):
    qpos = past + lax.broadcasted_iota(jnp.int32, (2 * t, 1), 0) % t
    vis_n = ((past + lax.broadcasted_iota(jnp.int32, (1, t), 1)) // CHUNK) <= (qpos // CHUNK)
    vis_c = (lax.broadcasted_iota(jnp.int32, (1, past), 1) // CHUNK) <= (qpos // CHUNK)
    scores = []
    for hh in range(hps):
        lanes = slice(hh * HEAD_SLAB, (hh + 1) * HEAD_SLAB)
        qq = _split_halves(q_ref[:, lanes])
        sn = lax.dot_general(qq, kn_ref[:, lanes], _NT, preferred_element_type=F32)
        sn = jnp.where(vis_n, sn + bn_ref[2 * hh:2 * hh + 2].reshape(2 * t, t), NEG_INF)
        sc = jnp.dot(qq, kct_ref[0, lanes, :].astype(BF16), preferred_element_type=F32)
        sc = jnp.where(vis_c, sc + bc_ref[2 * hh:2 * hh + 2].reshape(2 * t, past), NEG_INF)
        scores.append((sn, sc))
    for hh, (sn, sc) in enumerate(scores):
        lanes = slice(hh * HEAD_SLAB, (hh + 1) * HEAD_SLAB)
        m = jnp.maximum(jnp.max(sn, axis=-1, keepdims=True), jnp.max(sc, axis=-1, keepdims=True))
        pn = jnp.exp2(sn - m)
        pc = jnp.exp2(sc - m)
        l = jnp.sum(pn, axis=-1, keepdims=True) + jnp.sum(pc, axis=-1, keepdims=True)
        acc = (jnp.dot(pn.astype(BF16), vn_ref[:, lanes], preferred_element_type=F32)
               + jnp.dot(pc.astype(BF16), vc_ref[0, :, lanes].astype(BF16), preferred_element_type=F32))
        o_ref[:, lanes] = _da_finish(l, acc, t, lam_ref[0, 0], lam_init, subln_ref[...], sg_ref[:, lanes])


def _da_sample(q, ckt, cv, kn, vn, sg, bias_c, bias_n, lam, subln, batch, t, lam_init):
    tt, width = q.shape
    hps = 4
    nh = width // HEAD_SLAB // hps
    w = hps * HEAD_SLAB
    past = cv.shape[1]
    return pl.pallas_call(
        functools.partial(_da_sample_kernel, t=t, past=past, lam_init=lam_init, hps=hps),
        grid=(batch, nh),
        in_specs=[
            pl.BlockSpec(memory_space=pltpu.SMEM),
            pl.BlockSpec((t, w), lambda b, h: (b, h)),
            pl.BlockSpec((1, w, past), lambda b, h: (b, h, 0)),
            pl.BlockSpec((1, past, w), lambda b, h: (b, 0, h)),
            pl.BlockSpec((t, w), lambda b, h: (b, h)),
            pl.BlockSpec((t, w), lambda b, h: (b, h)),
            pl.BlockSpec((t, w), lambda b, h: (b, h)),
            pl.BlockSpec((2 * hps, t, past), lambda b, h: (h, 0, 0)),
            pl.BlockSpec((2 * hps, t, t), lambda b, h: (h, 0, 0)),
            pl.BlockSpec((1, HEAD_SLAB), lambda b, h: (0, 0)),
        ],
        out_specs=pl.BlockSpec((t, w), lambda b, h: (b, h)),
        out_shape=jax.ShapeDtypeStruct((tt, width), BF16),
        compiler_params=_cparams(("parallel", "parallel")),
        name="da_sample",
    )(lam, q, ckt, cv, kn, vn, sg, bias_c, bias_n, subln.reshape(1, HEAD_SLAB))


def _suffix_matrix(n):
    j = np.arange(n)[:, None]
    s = np.arange(n)[None, :]
    return jnp.asarray((j > s).astype(np.float32), dtype=BF16)


def _sb_logs(z, mask):
    lg = jnp.log(1.0 + jnp.exp2(-jnp.abs(z))) * LOG2E
    ls = jnp.minimum(z, 0.0) - lg
    lr = ls - z
    if mask is not None:
        lr = jnp.where(mask, lr, 0.0)
    return ls, lr


def _sb_suffix(lr, tmat):
    hi = lr.astype(BF16)
    lo = (lr - hi.astype(F32)).astype(BF16)
    return (jnp.dot(hi, tmat, preferred_element_type=F32)
            + jnp.dot(lo, tmat, preferred_element_type=F32))


def _sb_exp(ls, after, c, mask):
    arg = ls + after
    if c is not None:
        arg = arg + c
    a = jnp.exp2(arg)
    if mask is not None:
        a = jnp.where(mask, a, 0.0)
    return a.astype(BF16)


SB_DEAD_LOG2 = -160.0


def _sb_prompt_kernel(q_ref, kt_ref, vt_ref, sg_ref, tmat_ref, o_ref, acc_scr, rem_scr, *, tq, nq, hps):
    tmat = tmat_ref[...]
    r = lax.broadcasted_iota(jnp.int32, (2 * tq, tq), 0) % tq
    c = lax.broadcasted_iota(jnp.int32, (2 * tq, tq), 1)
    qqs = [_split_halves(q_ref[:, hh * HEAD_SLAB:(hh + 1) * HEAD_SLAB]) for hh in range(hps)]

    def body(qv):
        starts = [w * tq for w in range(qv, -1, -1)]

        def scores(hh, w):
            lanes = slice(hh * HEAD_SLAB, (hh + 1) * HEAD_SLAB)
            return jnp.dot(qqs[hh], kt_ref[0, lanes, starts[w]:starts[w] + tq], preferred_element_type=F32)

        def finish(pending, acc):
            ls, after, rem, mask, hh, w = pending
            lanes = slice(hh * HEAD_SLAB, (hh + 1) * HEAD_SLAB)
            a = _sb_exp(ls, after, rem, mask)
            pv = lax.dot_general(a, vt_ref[0, lanes, starts[w]:starts[w] + tq], _NT, preferred_element_type=F32)
            return pv if acc is None else acc + pv

        def run(ws, accs, rems):
            work = [(hh, w) for w in ws for hh in range(hps)]
            ahead = 2 * hps
            zs = {i: scores(*work[i]) for i in range(min(ahead, len(work)))}
            accs, rems = list(accs), list(rems)
            pending = None
            for i, (hh, w) in enumerate(work):
                mask = (c < r) if w == 0 else None
                ls, lr = _sb_logs(zs.pop(i), mask)
                if i + ahead < len(work):
                    zs[i + ahead] = scores(*work[i + ahead])
                if pending is not None:
                    accs[pending[4]] = finish(pending, accs[pending[4]])
                pending = (ls, _sb_suffix(lr, tmat), rems[hh], mask, hh, w)
                tot = jnp.sum(lr, axis=-1, keepdims=True)
                rems[hh] = tot if rems[hh] is None else rems[hh] + tot
            accs[pending[4]] = finish(pending, accs[pending[4]])
            return accs, rems

        ws = list(range(len(starts)))
        accs, rems = run(ws[:2], [None] * hps, [None] * hps)
        if len(ws) > 2:
            alive = None
            for hh in range(hps):
                acc_scr[hh] = accs[hh]
                rem_scr[hh] = rems[hh]
                mx = jnp.max(rems[hh])
                alive = mx if alive is None else jnp.maximum(alive, mx)

            @pl.when(alive > SB_DEAD_LOG2)
            def _():
                accs2, _ = run(ws[2:], [acc_scr[hh] for hh in range(hps)], [rem_scr[hh] for hh in range(hps)])
                for hh in range(hps):
                    acc_scr[hh] = accs2[hh]

            accs = [acc_scr[hh] for hh in range(hps)]
        for hh in range(hps):
            lanes = slice(hh * HEAD_SLAB, (hh + 1) * HEAD_SLAB)
            o_ref[:, lanes] = (sg_ref[:, lanes].astype(F32) * _join_halves(accs[hh], tq)).astype(BF16)

    _per_q_block(pl.program_id(2), nq, body)


def _sb_prompt(q, kt, vt, sg, batch, seq, tq):
    t, width = q.shape
    hps = 2
    nh = width // HEAD_SLAB // hps
    w = hps * HEAD_SLAB
    nq = seq // tq
    return pl.pallas_call(
        functools.partial(_sb_prompt_kernel, tq=tq, nq=nq, hps=hps),
        grid=(batch, nh, nq),
        in_specs=[
            pl.BlockSpec((tq, w), lambda b, h, i: (b * nq + i, h)),
            pl.BlockSpec((1, w, seq), lambda b, h, i: (b, h, 0)),
            pl.BlockSpec((1, w, seq), lambda b, h, i: (b, h, 0)),
            pl.BlockSpec((tq, w), lambda b, h, i: (b * nq + i, h)),
            pl.BlockSpec((tq, tq), lambda b, h, i: (0, 0)),
        ],
        out_specs=pl.BlockSpec((tq, w), lambda b, h, i: (b * nq + i, h)),
        out_shape=jax.ShapeDtypeStruct((t, width), BF16),
        scratch_shapes=[pltpu.VMEM((hps, 2 * tq, HEAD_SLAB), F32), pltpu.VMEM((hps, 2 * tq, 1), F32)],
        compiler_params=_cparams(("parallel", "parallel", "arbitrary")),
        name="sb_prompt",
    )(q, kt, vt, sg, _suffix_matrix(tq))


def _sb_sample_kernel(q_ref, kct_ref, vct_ref, kn_ref, vn_ref, sg_ref, tmat_ref, o_ref, *, t, tn, tk, past, hps):
    tmat = tmat_ref[...]
    pad = jnp.zeros((tn - t, HEAD_SLAB), BF16)
    r = lax.broadcasted_iota(jnp.int32, (2 * t, tn), 0) % t
    c = lax.broadcasted_iota(jnp.int32, (2 * t, tn), 1)
    blocks = []
    for hh in range(hps):
        lanes = slice(hh * HEAD_SLAB, (hh + 1) * HEAD_SLAB)
        qq = _split_halves(q_ref[:, lanes])
        kn = jnp.concatenate([kn_ref[:, lanes], pad], axis=0)
        z = lax.dot_general(qq, kn, _NT, preferred_element_type=F32)
        blocks.append(dict(hh=hh, z=z, mask=c < r, tm=tmat[:tn, :tn], j=None))
        for j in reversed(range(past // tk)):
            z = jnp.dot(qq, kct_ref[0, lanes, j * tk:(j + 1) * tk].astype(BF16), preferred_element_type=F32)
            blocks.append(dict(hh=hh, z=z, mask=None, tm=tmat, j=j))
    rem = [None] * hps
    for bk in blocks:
        bk["ls"], bk["lr"] = _sb_logs(bk.pop("z"), bk["mask"])
        bk["rem"] = rem[bk["hh"]]
        tot = jnp.sum(bk["lr"], axis=-1, keepdims=True)
        rem[bk["hh"]] = tot if bk["rem"] is None else bk["rem"] + tot
    for bk in blocks:
        bk["after"] = _sb_suffix(bk.pop("lr"), bk["tm"])
    acc = [None] * hps
    for bk in blocks:
        hh, j = bk["hh"], bk["j"]
        lanes = slice(hh * HEAD_SLAB, (hh + 1) * HEAD_SLAB)
        a = _sb_exp(bk["ls"], bk["after"], bk["rem"], bk["mask"])
        if j is None:
            vn = jnp.concatenate([vn_ref[:, lanes], pad], axis=0)
            pv = jnp.dot(a, vn, preferred_element_type=F32)
        else:
            vj = vct_ref[0, lanes, j * tk:(j + 1) * tk].astype(BF16)
            pv = lax.dot_general(a, vj, _NT, preferred_element_type=F32)
        acc[hh] = pv if acc[hh] is None else acc[hh] + pv
    for hh in range(hps):
        lanes = slice(hh * HEAD_SLAB, (hh + 1) * HEAD_SLAB)
        o_ref[:, lanes] = (sg_ref[:, lanes].astype(F32) * _join_halves(acc[hh], t)).astype(BF16)


def _sb_sample(q, ckt, cvt, kn, vn, sg, batch, t, tk):
    tt, width = q.shape
    hps = 4
    nh = width // HEAD_SLAB // hps
    w = hps * HEAD_SLAB
    past = ckt.shape[2]
    tk = min(tk, past)
    assert past % tk == 0
    tn = HEAD_SLAB
    assert t <= tn <= tk
    return pl.pallas_call(
        functools.partial(_sb_sample_kernel, t=t, tn=tn, tk=tk, past=past, hps=hps),
        grid=(batch, nh),
        in_specs=[
            pl.BlockSpec((t, w), lambda b, h: (b, h)),
            pl.BlockSpec((1, w, past), lambda b, h: (b, h, 0)),
            pl.BlockSpec((1, w, past), lambda b, h: (b, h, 0)),
            pl.BlockSpec((t, w), lambda b, h: (b, h)),
            pl.BlockSpec((t, w), lambda b, h: (b, h)),
            pl.BlockSpec((t, w), lambda b, h: (b, h)),
            pl.BlockSpec((tk, tk), lambda b, h: (0, 0)),
        ],
        out_specs=pl.BlockSpec((t, w), lambda b, h: (b, h)),
        out_shape=jax.ShapeDtypeStruct((tt, width), BF16),
        compiler_params=_cparams(("parallel", "parallel")),
        name="sb_sample",
    )(q, ckt, cvt, kn, vn, sg, _suffix_matrix(tk))


def _sw_group(sink_ref, q_slabs, kk, vv, bias, valid, kvh, group, t):
    qst = jnp.concatenate([_split_halves(qs) for qs in q_slabs], axis=0)
    s = lax.dot_general(qst, kk, _NT, preferred_element_type=F32) + bias
    if valid is not None:
        s = jnp.where(valid, s, NEG_INF)
    row = lax.broadcasted_iota(jnp.int32, (group * t, 1), 0)
    sk = jnp.zeros((group * t, 1), F32)
    for g in range(group):
        sk = jnp.where(row // t == g, sink_ref[kvh * group + g] * LOG2E, sk)
    m = jnp.maximum(jnp.max(s, axis=-1, keepdims=True), sk)
    p = jnp.exp2(s - m)
    w = p / (jnp.sum(p, axis=-1, keepdims=True) + jnp.exp2(sk - m))
    o = jnp.dot(w.astype(BF16), vv, preferred_element_type=F32)
    return [_join_halves(o[2 * i * t:(2 * i + 2) * t], t) for i in range(group // 2)]


def _sw_prompt_kernel(sink_ref, q_ref, kp_ref, kc_ref, vp_ref, vc_ref, sg_ref, bias_ref, o_ref,
                      *, tq, n_kv, group):
    band = (WIN_CHUNKS + 1) * CHUNK
    kcat = jnp.concatenate([kp_ref[...], kc_ref[...]], axis=0)
    vcat = jnp.concatenate([vp_ref[...], vc_ref[...]], axis=0)
    col = lax.broadcasted_iota(jnp.int32, (1, band), 1)

    def body(first):
        for cc in range(tq // CHUNK):
            r0 = cc * CHUNK
            valid = (col >= WINDOW - r0) if first and r0 < WINDOW else None
            for kvh in range(n_kv):
                kk = kcat[r0:r0 + band, kvh * HEAD_SLAB:(kvh + 1) * HEAD_SLAB]
                vv = vcat[r0:r0 + band, kvh * HEAD_SLAB:(kvh + 1) * HEAD_SLAB]
                s0 = kvh * (group // 2)
                q_slabs = [q_ref[r0:r0 + CHUNK, (s0 + i) * HEAD_SLAB:(s0 + i + 1) * HEAD_SLAB]
                           for i in range(group // 2)]
                bias = bias_ref[kvh * group:(kvh + 1) * group].reshape(group * CHUNK, band)
                outs = _sw_group(sink_ref, q_slabs, kk, vv, bias, valid, kvh, group, CHUNK)
                for i, o in enumerate(outs):
                    lanes = slice((s0 + i) * HEAD_SLAB, (s0 + i + 1) * HEAD_SLAB)
                    sg = sg_ref[r0:r0 + CHUNK, lanes].astype(F32)
                    o_ref[r0:r0 + CHUNK, lanes] = (sg * o).astype(BF16)

    qi = pl.program_id(1)
    pl.when(qi == 0)(functools.partial(body, True))
    pl.when(qi > 0)(functools.partial(body, False))


def _sw_prompt(q, kd, vd, sg, bias, sinks, batch, seq, tq, n_kv, group):
    t, width = q.shape
    nq = seq // tq
    wpb = tq // WINDOW
    kvw = kd.shape[1]
    band = (WIN_CHUNKS + 1) * CHUNK
    prev = lambda b, i: (jnp.maximum((b * nq + i) * wpb - 1, 0), 0)
    cur = lambda b, i: (b * nq + i, 0)
    return pl.pallas_call(
        functools.partial(_sw_prompt_kernel, tq=tq, n_kv=n_kv, group=group),
        grid=(batch, nq),
        in_specs=[
            pl.BlockSpec(memory_space=pltpu.SMEM),
            pl.BlockSpec((tq, width), cur),
            pl.BlockSpec((WINDOW, kvw), prev),
            pl.BlockSpec((tq, kvw), cur),
            pl.BlockSpec((WINDOW, kvw), prev),
            pl.BlockSpec((tq, kvw), cur),
            pl.BlockSpec((tq, width), cur),
            pl.BlockSpec((n_kv * group, CHUNK, band), lambda b, i: (0, 0, 0)),
        ],
        out_specs=pl.BlockSpec((tq, width), cur),
        out_shape=jax.ShapeDtypeStruct((t, width), BF16),
        compiler_params=_cparams(("parallel", "arbitrary")),
        name="sw_prompt",
    )(sinks, q, kd, kd, vd, vd, sg, bias)


def _sw_sample_kernel(sink_ref, q_ref, k_ref, v_ref, sg_ref, bias_ref, o_ref, *, t, past, wb, n_kv, group):
    band = wb + t
    qpos = past + lax.broadcasted_iota(jnp.int32, (group * t, 1), 0) % t
    kpos = past - wb + lax.broadcasted_iota(jnp.int32, (1, band), 1)
    qc = qpos // CHUNK
    kc = kpos // CHUNK
    valid = (kc <= qc) & (kc >= qc - WIN_CHUNKS)
    for kvh in range(n_kv):
        kk = k_ref[0, :, kvh * HEAD_SLAB:(kvh + 1) * HEAD_SLAB]
        vv = v_ref[0, :, kvh * HEAD_SLAB:(kvh + 1) * HEAD_SLAB]
        s0 = kvh * (group // 2)
        q_slabs = [q_ref[:, (s0 + i) * HEAD_SLAB:(s0 + i + 1) * HEAD_SLAB] for i in range(group // 2)]
        bias = bias_ref[kvh * group:(kvh + 1) * group].reshape(group * t, band)
        outs = _sw_group(sink_ref, q_slabs, kk, vv, bias, valid, kvh, group, t)
        for i, o in enumerate(outs):
            lanes = slice((s0 + i) * HEAD_SLAB, (s0 + i + 1) * HEAD_SLAB)
            o_ref[:, lanes] = (sg_ref[:, lanes].astype(F32) * o).astype(BF16)


def _sw_sample(q, kd, vd, sg, bias, sinks, batch, t, past, wb, n_kv, group):
    tt, width = q.shape
    band, kvw = kd.shape[1:]
    return pl.pallas_call(
        functools.partial(_sw_sample_kernel, t=t, past=past, wb=wb, n_kv=n_kv, group=group),
        grid=(batch,),
        in_specs=[
            pl.BlockSpec(memory_space=pltpu.SMEM),
            pl.BlockSpec((t, width), lambda b: (b, 0)),
            pl.BlockSpec((1, band, kvw), lambda b: (b, 0, 0)),
            pl.BlockSpec((1, band, kvw), lambda b: (b, 0, 0)),
            pl.BlockSpec((t, width), lambda b: (b, 0)),
            pl.BlockSpec((n_kv * group, t, band), lambda b: (0, 0, 0)),
        ],
        out_specs=pl.BlockSpec((t, width), lambda b: (b, 0)),
        out_shape=jax.ShapeDtypeStruct((tt, width), BF16),
        compiler_params=_cparams(("parallel",)),
        name="sw_sample",
    )(sinks, q, kd, vd, sg, bias)


TM_PROJ = 512
TQ_ATTN = 256


def _feature_major(a):
    nd = a.ndim
    a = jnp.transpose(a, (0,) + tuple(range(2, nd)) + (1,))
    return a.reshape(a.shape[0], -1, a.shape[-1])


def _position_major(a, feat_shape):
    b, _, p = a.shape
    a = a.reshape((b,) + tuple(feat_shape) + (p,))
    nd = a.ndim
    return jnp.transpose(a, (0, nd - 1) + tuple(range(1, nd - 1)))


def _project(x, res, norm, w_nat, plan, dtypes, tm, **kw):
    outs = _inproj(x, norm, w_nat, plan, dtypes, tm, res=res, **kw)
    return (x, outs) if res is None else (outs[0], outs[1:])


def _da_layer(xp, xs, res_p, res_s, ck, cv, table, norm, w_in, lam_params, subln, lam_init, dims):
    batch, seq, dec_batch, t_dec, past, d = dims
    br = d
    wq, wk, wv, wg = (w_in[:, i * br:(i + 1) * br] for i in range(4))
    w_p = jnp.concatenate([wq, wv, wg], axis=1).astype(BF16)
    plan_p = [(0, br, "qscale2"), (br, br, "none"), (2 * br, br, "silu"), (br, br, "none")]
    xp, (qp, vp, sgp, vpf, ktf, kt) = _project(
        xp, res_p, norm, w_p, plan_p, [BF16, BF16, BF16, F32], TM_PROJ,
        w_t=wk.T.astype(BF16), plan_t=[(0, br), (0, br)], dtypes_t=[F32, BF16], seq=seq)
    plan_s = [(0, br, "qscale2"), (br, br, "none"), (2 * br, br, "none"), (3 * br, br, "silu"),
              (br, br, "none"), (2 * br, br, "none")]
    xs, (qs, ks, vs, sgs, ksf, vsf) = _project(xs, res_s, norm, w_in.astype(BF16), plan_s,
                                               [BF16, BF16, BF16, BF16, F32, F32], TM_PROJ)
    lam = _diff_lambda(lam_params, lam_init)
    tq = min(TQ_ATTN, seq)
    assert seq % tq == 0 and tq >= MAX_DISTANCE and tq % CHUNK == 0
    assert past == ck.shape[1] and past % CHUNK == 0 and t_dec <= CHUNK
    i = jnp.arange(tq)[:, None]
    j = jnp.arange(tq)[None, :]
    rel_p = jnp.concatenate([j - i, j - tq - i], axis=0)
    nbh = table.shape[1]
    bias_p = _bias_tiles(table, rel_p, LOG2E).reshape(nbh, 2, tq, tq)
    og_p = _da_prompt(qp, kt, vp, sgp, bias_p, table, lam, subln, batch, seq, tq, lam_init)
    q_pos = past + jnp.arange(t_dec)[:, None]
    bias_c = _bias_tiles(table, jnp.arange(past)[None, :] - q_pos, LOG2E)
    bias_n = _bias_tiles(table, past + jnp.arange(t_dec)[None, :] - q_pos, LOG2E)
    og_s = _da_sample(qs, _feature_major(ck), cv.reshape(dec_batch, past, br), ks, vs, sgs,
                      bias_c, bias_n, lam, subln, dec_batch, t_dec, lam_init)
    state = (_position_major(ktf, ck.shape[2:]), vpf.reshape((batch, seq) + cv.shape[2:]),
             ksf.reshape((dec_batch, t_dec) + ck.shape[2:]), vsf.reshape((dec_batch, t_dec) + cv.shape[2:]))
    return xp, xs, og_p, og_s, state


def _sb_layer(xp, xs, res_p, res_s, ck, cv, norm, w_in, dims):
    batch, seq, dec_batch, t_dec, past, d = dims
    br = d
    wq, wk, wv, wg = (w_in[:, i * br:(i + 1) * br] for i in range(4))
    w_p = jnp.concatenate([wq, wg], axis=1).astype(BF16)
    w_t = jnp.concatenate([wk, wv], axis=1).T.astype(BF16)
    xp, (qp, sgp, ktf, kt, vtf, vt) = _project(
        xp, res_p, norm, w_p, [(0, br, "qscale2"), (br, br, "silu")], [BF16, BF16], TM_PROJ,
        w_t=w_t, plan_t=[(0, br), (0, br), (br, br), (br, br)], dtypes_t=[F32, BF16, F32, BF16], seq=seq)
    plan_s = [(0, br, "qscale2"), (br, br, "none"), (2 * br, br, "none"), (3 * br, br, "silu"),
              (br, br, "none"), (2 * br, br, "none")]
    xs, (qs, ks, vs, sgs, ksf, vsf) = _project(xs, res_s, norm, w_in.astype(BF16), plan_s,
                                               [BF16, BF16, BF16, BF16, F32, F32], TM_PROJ)
    tq = min(TQ_ATTN, seq)
    assert seq % tq == 0 and past == ck.shape[1]
    og_p = _sb_prompt(qp, kt, vt, sgp, batch, seq, tq)
    og_s = _sb_sample(qs, _feature_major(ck), _feature_major(cv), ks, vs, sgs, dec_batch, t_dec, TQ_ATTN)
    state = (_position_major(ktf, ck.shape[2:]), _position_major(vtf, cv.shape[2:]),
             ksf.reshape((dec_batch, t_dec) + ck.shape[2:]), vsf.reshape((dec_batch, t_dec) + cv.shape[2:]))
    return xp, xs, og_p, og_s, state


def _dup_heads(a, n_kv, hd):
    lead = a.shape[:-1]
    a = a.reshape(lead + (n_kv, 1, hd))
    return jnp.broadcast_to(a, lead + (n_kv, 2, hd)).reshape(lead + (n_kv * 2 * hd,))


def _sw_layer(xp, xs, res_p, res_s, ck, cv, table, norm, w_in, sinks, dims):
    batch, seq, dec_batch, t_dec, past, d = dims
    br = d
    wb, n_kv, hd = ck.shape[1:]
    assert hd == HALF
    kvw = n_kv * hd
    group = br // hd // n_kv
    assert group % 2 == 0
    wq, wk, wv, wg = (w_in[:, :br], w_in[:, br:br + kvw], w_in[:, br + kvw:br + 2 * kvw],
                      w_in[:, br + 2 * kvw:])
    w_p = jnp.concatenate([wq, _dup_heads(wk, n_kv, hd), _dup_heads(wv, n_kv, hd), wg], axis=1).astype(BF16)
    plan_p = [(0, br, "qscale2"), (br, 2 * kvw, "none"), (br + 2 * kvw, 2 * kvw, "none"),
              (br + 4 * kvw, br, "silu")]
    xp, (qp, kdp, vdp, sgp) = _project(xp, res_p, norm, w_p, plan_p, [BF16] * 4, TM_PROJ)
    keep = min(WINDOW, seq)
    w_kv_t = jnp.concatenate([wk, wv], axis=1).T.astype(BF16)
    x_tail = xp.reshape(batch, seq, d)[:, seq - keep:].reshape(batch * keep, d)
    ktf, vtf = _inproj(x_tail, norm, None, [], [], keep, w_t=w_kv_t,
                       plan_t=[(0, kvw), (kvw, kvw)], dtypes_t=[F32, F32], seq=keep)
    plan_s = [(0, br, "qscale2"), (br, kvw, "none"), (br + kvw, kvw, "none"), (br + 2 * kvw, br, "silu")]
    xs, (qs, ksf, vsf, sgs) = _project(xs, res_s, norm, w_in.astype(BF16), plan_s, [BF16, F32, F32, BF16],
                                       TM_PROJ)

    tq = min(TQ_ATTN, seq)
    assert seq % tq == 0 and tq % WINDOW == 0
    band = (WIN_CHUNKS + 1) * CHUNK
    rel_p = (jnp.arange(band)[None, :] - WIN_CHUNKS * CHUNK) - jnp.arange(CHUNK)[:, None]
    bias_p = _bias_tiles(table, rel_p, LOG2E)
    og_p = _sw_prompt(qp, kdp, vdp, sgp, bias_p, sinks, batch, seq, tq, n_kv, group)

    k_all = jnp.concatenate([ck, ksf.reshape(dec_batch, t_dec, n_kv, hd)], axis=1)
    v_all = jnp.concatenate([cv, vsf.reshape(dec_batch, t_dec, n_kv, hd)], axis=1)
    kd_s = _dup_heads(k_all.reshape(dec_batch, wb + t_dec, kvw), n_kv, hd).astype(BF16)
    vd_s = _dup_heads(v_all.reshape(dec_batch, wb + t_dec, kvw), n_kv, hd).astype(BF16)
    rel_s = (past - wb + jnp.arange(wb + t_dec)[None, :]) - (past + jnp.arange(t_dec)[:, None])
    bias_s = _bias_tiles(table, rel_s, LOG2E)
    og_s = _sw_sample(qs, kd_s, vd_s, sgs, bias_s, sinks, dec_batch, t_dec, past, wb, n_kv, group)
    state = (_position_major(ktf, (n_kv, hd)), _position_major(vtf, (n_kv, hd)),
             k_all[:, -wb:], v_all[:, -wb:])
    return xp, xs, og_p, og_s, state


def kernel(x_prompt, x_sample, cache_k_0, cache_v_0, cache_k_1, cache_v_1, cache_k_2, cache_v_2,
           cache_k_3, cache_v_3, rel_bias_table,
           norm_0, w_in_0, w_out_0, da_lambda_0, da_subln_0,
           norm_1, w_in_1, w_out_1,
           norm_2, w_in_2, w_out_2, sw_sinks_2,
           norm_3, w_in_3, w_out_3, da_lambda_3, da_subln_3,
           final_norm):
    batch, seq, d = x_prompt.shape
    dec_batch, t_dec, _ = x_sample.shape
    dims = (batch, seq, dec_batch, t_dec, cache_k_0.shape[1], d)
    xp = x_prompt.reshape(batch * seq, d)
    xs = x_sample.reshape(dec_batch * t_dec, d)
    layers = [
        ("da", cache_k_0, cache_v_0, norm_0, w_in_0, w_out_0, (da_lambda_0, da_subln_0)),
        ("sb", cache_k_1, cache_v_1, norm_1, w_in_1, w_out_1, ()),
        ("sw", cache_k_2, cache_v_2, norm_2, w_in_2, w_out_2, (sw_sinks_2,)),
        ("da", cache_k_3, cache_v_3, norm_3, w_in_3, w_out_3, (da_lambda_3, da_subln_3)),
    ]
    states = []
    res_p = res_s = None
    for i, (kind, ck, cv, norm, w_in, w_out, extra) in enumerate(layers):
        if kind == "da":
            lam_init = 0.8 - 0.6 * math.exp(-0.3 * i)
            xp, xs, og_p, og_s, st = _da_layer(xp, xs, res_p, res_s, ck, cv, rel_bias_table, norm, w_in,
                                               extra[0], extra[1], lam_init, dims)
        elif kind == "sb":
            xp, xs, og_p, og_s, st = _sb_layer(xp, xs, res_p, res_s, ck, cv, norm, w_in, dims)
        else:
            xp, xs, og_p, og_s, st = _sw_layer(xp, xs, res_p, res_s, ck, cv, rel_bias_table, norm, w_in,
                                               extra[0], dims)
        w_o = w_out.astype(BF16)
        res_p, res_s = (og_p, w_o), (og_s, w_o)
        states.append(st)
    xp = _outproj(xp, res_p[0], res_p[1], final_norm, 2 * TM_PROJ)
    xs = _outproj(xs, res_s[0], res_s[1], final_norm, 2 * TM_PROJ)
    out = [xp.reshape(batch, seq, d), xs.reshape(dec_batch, t_dec, d)]
    for st in states:
        out.extend(st)
    return tuple(out)
```

```python
import functools
import math

import jax
import jax.numpy as jnp
import numpy as np
from jax import lax
from jax.experimental import pallas as pl
from jax.experimental.pallas import tpu as pltpu

F32 = jnp.float32
BF16 = jnp.bfloat16

EPS = 1e-6
NEG_INF = -1e30
LOG2E = math.log2(math.e)
CHUNK = 64
WINDOW = 128
WIN_CHUNKS = WINDOW // CHUNK
N_BUCKETS = 32
MAX_DISTANCE = 128
FAR_BUCKET = N_BUCKETS // 2 - 1
HEAD_SLAB = 128
HALF = HEAD_SLAB // 2
VMEM_LIMIT = 48 * 1024 * 1024

_NT = (((1,), (1,)), ((), ()))


def _cparams(sem):
    return pltpu.CompilerParams(dimension_semantics=sem, vmem_limit_bytes=VMEM_LIMIT)


def _inproj_kernel(*refs, plan, plan_t, has_nat, has_res):
    x_ref, refs = refs[0], refs[1:]
    x = x_ref[...]
    if has_res:
        (og_ref, wo_ref), refs = refs[:2], refs[2:]
        x = x + jnp.dot(og_ref[...], wo_ref[...], preferred_element_type=F32)
    g_ref, refs = refs[0], refs[1:]
    if has_nat:
        w_ref, refs = refs[0], refs[1:]
    if plan_t:
        wt_ref, refs = refs[0], refs[1:]
    if has_res:
        refs[0][...] = x
        refs = refs[1:]
    y = x * lax.rsqrt(jnp.mean(x * x, axis=-1, keepdims=True) + EPS)
    xn = (y * g_ref[...]).astype(BF16)
    if has_nat:
        h = jnp.dot(xn, w_ref[...], preferred_element_type=F32)
        for o_ref, (c0, width, kind) in zip(refs, plan):
            t = h[:, c0:c0 + width]
            if kind == "qscale2":
                t = t * (HALF ** -0.5 * LOG2E)
            elif kind == "silu":
                t = t * jax.nn.sigmoid(t)
            o_ref[...] = t.astype(o_ref.dtype)
        refs = refs[len(plan):]
    if plan_t:
        ht = lax.dot_general(wt_ref[...], xn, _NT, preferred_element_type=F32)
        for o_ref, (r0, nrows) in zip(refs, plan_t):
            o_ref[0] = ht[r0:r0 + nrows].astype(o_ref.dtype)


def _inproj(x2d, g, w_nat, plan, dtypes, tm, w_t=None, plan_t=(), dtypes_t=(), seq=None, res=None):
    t, d = x2d.shape
    tm = min(tm, t)
    assert t % tm == 0
    in_specs = [pl.BlockSpec((tm, d), lambda i: (i, 0))]
    args = [x2d]
    out_shape, out_specs = [], []
    if res is not None:
        og, w_out = res
        in_specs += [pl.BlockSpec((tm, og.shape[1]), lambda i: (i, 0)), pl.BlockSpec(w_out.shape, lambda i: (0, 0))]
        args += [og, w_out]
        out_shape.append(jax.ShapeDtypeStruct((t, d), F32))
        out_specs.append(pl.BlockSpec((tm, d), lambda i: (i, 0)))
    in_specs.append(pl.BlockSpec((1, d), lambda i: (0, 0)))
    args.append(g.reshape(1, d))
    if w_nat is not None:
        in_specs.append(pl.BlockSpec(w_nat.shape, lambda i: (0, 0)))
        args.append(w_nat)
        out_shape += [jax.ShapeDtypeStruct((t, width), dt) for (_, width, _), dt in zip(plan, dtypes)]
        out_specs += [pl.BlockSpec((tm, width), lambda i: (i, 0)) for (_, width, _) in plan]
    if plan_t:
        assert seq % tm == 0 and t % seq == 0
        tps = seq // tm
        in_specs.append(pl.BlockSpec(w_t.shape, lambda i: (0, 0)))
        args.append(w_t)
        out_shape += [jax.ShapeDtypeStruct((t // seq, nrows, seq), dt) for (_, nrows), dt in zip(plan_t, dtypes_t)]
        out_specs += [pl.BlockSpec((1, nrows, tm), lambda i: (i // tps, 0, i % tps)) for (_, nrows) in plan_t]
    return pl.pallas_call(
        functools.partial(_inproj_kernel, plan=tuple(plan), plan_t=tuple(plan_t), has_nat=w_nat is not None,
                          has_res=res is not None),
        grid=(t // tm,),
        in_specs=in_specs,
        out_specs=out_specs,
        out_shape=out_shape,
        compiler_params=_cparams(("parallel",)),
        name="inproj",
    )(*args)


def _outproj_kernel(x_ref, og_ref, w_ref, g_ref, o_ref):
    x = x_ref[...] + jnp.dot(og_ref[...], w_ref[...], preferred_element_type=F32)
    y = x * lax.rsqrt(jnp.mean(x * x, axis=-1, keepdims=True) + EPS)
    o_ref[...] = y * g_ref[...]


def _outproj(x2d, og, w_bf16, final_g, tm):
    t, d = x2d.shape
    br = og.shape[1]
    tm = min(tm, t)
    assert t % tm == 0
    return pl.pallas_call(
        _outproj_kernel,
        grid=(t // tm,),
        in_specs=[
            pl.BlockSpec((tm, d), lambda i: (i, 0)),
            pl.BlockSpec((tm, br), lambda i: (i, 0)),
            pl.BlockSpec((br, d), lambda i: (0, 0)),
            pl.BlockSpec((1, d), lambda i: (0, 0)),
        ],
        out_specs=pl.BlockSpec((tm, d), lambda i: (i, 0)),
        out_shape=jax.ShapeDtypeStruct((t, d), F32),
        compiler_params=_cparams(("parallel",)),
        name="outproj",
    )(x2d, og, w_bf16, final_g.reshape(1, d))


def _rel_bucket(rel):
    nb = N_BUCKETS // 2
    max_exact = nb // 2
    n = jnp.abs(rel)
    nf = jnp.maximum(n, 1).astype(F32)
    large = max_exact + (jnp.log(nf / max_exact) / math.log(MAX_DISTANCE / max_exact)
                         * (nb - max_exact)).astype(jnp.int32)
    large = jnp.minimum(large, nb - 1)
    return jnp.where(rel > 0, nb, 0) + jnp.where(n < max_exact, n, large)


def _bias_kernel(tab_ref, idx_ref, o_ref, *, scale):
    h = pl.program_id(0)
    idx = idx_ref[...]
    acc = jnp.zeros(idx.shape, F32)
    for b in range(N_BUCKETS):
        acc = jnp.where(idx == b, tab_ref[b, h], acc)
    o_ref[0] = acc * scale if scale != 1.0 else acc


def _bias_tiles(table, rel, scale=1.0):
    idx = _rel_bucket(rel).astype(jnp.int32)
    r, c = idx.shape
    nh = table.shape[1]
    return pl.pallas_call(
        functools.partial(_bias_kernel, scale=scale),
        grid=(nh,),
        in_specs=[
            pl.BlockSpec(memory_space=pltpu.SMEM),
            pl.BlockSpec((r, c), lambda h: (0, 0)),
        ],
        out_specs=pl.BlockSpec((1, r, c), lambda h: (h, 0, 0)),
        out_shape=jax.ShapeDtypeStruct((nh, r, c), F32),
        compiler_params=_cparams(("arbitrary",)),
        name="bias_tiles",
    )(table, idx)


def _split_halves(q_bf16):
    qf = q_bf16.astype(F32)
    lane = lax.broadcasted_iota(jnp.int32, qf.shape, 1)
    qa = jnp.where(lane < HALF, qf, 0.0).astype(BF16)
    qb = jnp.where(lane >= HALF, qf, 0.0).astype(BF16)
    return jnp.concatenate([qa, qb], axis=0)


def _join_halves(o, t):
    lane = lax.broadcasted_iota(jnp.int32, (t, HEAD_SLAB), 1)
    return jnp.where(lane < HALF, o[:t], o[t:])


def _da_finish(l, acc, t, lam, lam_init, subln, sg):
    o = acc[:t] / l[:t] - lam * (acc[t:] / l[t:])
    o = o * lax.rsqrt(jnp.mean(o * o, axis=-1, keepdims=True) + EPS) * subln
    o = o * (1.0 - lam_init)
    return (sg.astype(F32) * o).astype(BF16)


def _lam_kernel(lp_ref, o_ref, *, lam_init):
    lp = lp_ref[...]
    a = jnp.sum(lp[0:1] * lp[1:2], axis=-1, keepdims=True)
    b = jnp.sum(lp[2:3] * lp[3:4], axis=-1, keepdims=True)
    o_ref[...] = jnp.exp(a) - jnp.exp(b) + lam_init


def _diff_lambda(lam_params, lam_init):
    return pl.pallas_call(
        functools.partial(_lam_kernel, lam_init=lam_init),
        out_shape=jax.ShapeDtypeStruct((1, 1), F32),
        name="diff_lambda",
    )(lam_params)


def _per_q_block(qi, nq, body):
    for qv in range(nq):
        pl.when(qi == qv)(functools.partial(body, qv))


def _da_prompt_kernel(tab_ref, lam_ref, q_ref, kt_ref, v_ref, sg_ref, bias_ref, subln_ref, o_ref,
                      *, tq, tkf, nq, lam_init, hps):
    hg = pl.program_id(1)
    rows = 2 * tq
    r = lax.broadcasted_iota(jnp.int32, (rows, tq), 0) % tq
    c = lax.broadcasted_iota(jnp.int32, (rows, tq), 1)
    row = lax.broadcasted_iota(jnp.int32, (rows, 1), 0)
    qqs, fars = [], []
    for hh in range(hps):
        h = hg * hps + hh
        qqs.append(_split_halves(q_ref[:, hh * HEAD_SLAB:(hh + 1) * HEAD_SLAB]))
        fars.append(LOG2E * jnp.where(row < tq, tab_ref[FAR_BUCKET, 2 * h], tab_ref[FAR_BUCKET, 2 * h + 1]))

    def body(qv):
        items = [(qv * tq, tq, "diag")]
        if qv >= 1:
            items.append(((qv - 1) * tq, tq, "prev"))
        nfar = max(qv - 1, 0) * tq
        for pos in range(0, nfar, tkf):
            items.append((pos, min(tkf, nfar - pos), "far"))
        work = [(hh, w) for w in range(len(items)) for hh in range(hps)]

        def scores(hh, w):
            st, n, kind = items[w]
            lanes = slice(hh * HEAD_SLAB, (hh + 1) * HEAD_SLAB)
            s = jnp.dot(qqs[hh], kt_ref[0, lanes, st:st + n], preferred_element_type=F32)
            if kind == "diag":
                s = s + bias_ref[2 * hh:2 * hh + 2, 0].reshape(rows, tq)
                s = jnp.where((c // CHUNK) <= (r // CHUNK), s, NEG_INF)
            elif kind == "prev":
                s = s + bias_ref[2 * hh:2 * hh + 2, 1].reshape(rows, tq)
            return s

        ahead = 3 * hps
        sc = {i: scores(*work[i]) for i in range(min(ahead, len(work)))}
        state = [dict(m=None, acc=None, far=False) for _ in range(hps)]
        for i, (hh, w) in enumerate(work):
            st, n, kind = items[w]
            z = state[hh]
            s = sc.pop(i)
            m = z["m"]
            if kind == "far" and not z["far"]:
                m = m - fars[hh]
                z["far"] = True
            bm = jnp.max(s, axis=-1, keepdims=True)
            if m is None:
                m_new = bm
            else:
                m_new = jnp.maximum(m, bm)
                alpha = jnp.exp2(m - m_new)
            p = jnp.exp2((s - m_new).astype(BF16))
            if i + ahead < len(work):
                sc[i + ahead] = scores(*work[i + ahead])
            lanes = slice(hh * HEAD_SLAB, (hh + 1) * HEAD_SLAB)
            vx = jnp.concatenate([v_ref[st:st + n, lanes], jnp.ones((n, HEAD_SLAB), BF16)], axis=1)
            pv = jnp.dot(p, vx, preferred_element_type=F32)
            z["acc"] = pv if m is None else alpha * z["acc"] + pv
            z["m"] = m_new
        for hh in range(hps):
            lanes = slice(hh * HEAD_SLAB, (hh + 1) * HEAD_SLAB)
            acc = state[hh]["acc"]
            o_ref[:, lanes] = _da_finish(acc[:, HEAD_SLAB:], acc[:, :HEAD_SLAB], tq, lam_ref[0, 0], lam_init,
                                         subln_ref[...], sg_ref[:, lanes])

    _per_q_block(pl.program_id(2), nq, body)


def _da_prompt(q, kt, v, sg, bias, table, lam, subln, batch, seq, tq, lam_init):
    t, width = q.shape
    hps = 2
    nh = width // HEAD_SLAB // hps
    nq = seq // tq
    w = hps * HEAD_SLAB
    return pl.pallas_call(
        functools.partial(_da_prompt_kernel, tq=tq, tkf=4 * tq, nq=nq, lam_init=lam_init, hps=hps),
        grid=(batch, nh, nq),
        in_specs=[
            pl.BlockSpec(memory_space=pltpu.SMEM),
            pl.BlockSpec(memory_space=pltpu.SMEM),
            pl.BlockSpec((tq, w), lambda b, h, i: (b * nq + i, h)),
            pl.BlockSpec((1, w, seq), lambda b, h, i: (b, h, 0)),
            pl.BlockSpec((seq, w), lambda b, h, i: (b, h)),
            pl.BlockSpec((tq, w), lambda b, h, i: (b * nq + i, h)),
            pl.BlockSpec((2 * hps, 2, tq, tq), lambda b, h, i: (h, 0, 0, 0)),
            pl.BlockSpec((1, HEAD_SLAB), lambda b, h, i: (0, 0)),
        ],
        out_specs=pl.BlockSpec((tq, w), lambda b, h, i: (b * nq + i, h)),
        out_shape=jax.ShapeDtypeStruct((t, width), BF16),
        compiler_params=_cparams(("parallel", "parallel", "arbitrary")),
        name="da_prompt",
    )(table, lam, q, kt, v, sg, bias, subln.reshape(1, HEAD_SLAB))


def _da_sample_kernel(lam_ref, q_ref, kct_ref, vc_ref, kn_ref, vn_ref, sg_ref, bc_ref, bn_ref,
                      subln_ref, o_ref, *, t, past, lam_init, hps):
    qpos = past + lax.broadcasted_iota(jnp.int32, (2 * t, 1), 0) % t
    vis_n = ((past + lax.broadcasted_iota(jnp.int32, (1, t), 1)) // CHUNK) <= (qpos // CHUNK)
    vis_c = (lax.broadcasted_iota(jnp.int32, (1, past), 1) // CHUNK) <= (qpos // CHUNK)
    scores = []
    for hh in range(hps):
        lanes = slice(hh * HEAD_SLAB, (hh + 1) * HEAD_SLAB)
        qq = _split_halves(q_ref[:, lanes])
        sn = lax.dot_general(qq, kn_ref[:, lanes], _NT, preferred_element_type=F32)
        sn = jnp.where(vis_n, sn + bn_ref[2 * hh:2 * hh + 2].reshape(2 * t, t), NEG_INF)
        sc = jnp.dot(qq, kct_ref[0, lanes, :].astype(BF16), preferred_element_type=F32)
        sc = jnp.where(vis_c, sc + bc_ref[2 * hh:2 * hh + 2].reshape(2 * t, past), NEG_INF)
        scores.append((sn, sc))
    for hh, (sn, sc) in enumerate(scores):
        lanes = slice(hh * HEAD_SLAB, (hh + 1) * HEAD_SLAB)
        m = jnp.maximum(jnp.max(sn, axis=-1, keepdims=True), jnp.max(sc, axis=-1, keepdims=True))
        pn = jnp.exp2(sn - m)
        pc = jnp.exp2(sc - m)
        l = jnp.sum(pn, axis=-1, keepdims=True) + jnp.sum(pc, axis=-1, keepdims=True)
        acc = (jnp.dot(pn.astype(BF16), vn_ref[:, lanes], preferred_element_type=F32)
               + jnp.dot(pc.astype(BF16), vc_ref[0, :, lanes].astype(BF16), preferred_element_type=F32))
        o_ref[:, lanes] = _da_finish(l, acc, t, lam_ref[0, 0], lam_init, subln_ref[...], sg_ref[:, lanes])


def _da_sample(q, ckt, cv, kn, vn, sg, bias_c, bias_n, lam, subln, batch, t, lam_init):
    tt, width = q.shape
    hps = 4
    nh = width // HEAD_SLAB // hps
    w = hps * HEAD_SLAB
    past = cv.shape[1]
    return pl.pallas_call(
        functools.partial(_da_sample_kernel, t=t, past=past, lam_init=lam_init, hps=hps),
        grid=(batch, nh),
        in_specs=[
            pl.BlockSpec(memory_space=pltpu.SMEM),
            pl.BlockSpec((t, w), lambda b, h: (b, h)),
            pl.BlockSpec((1, w, past), lambda b, h: (b, h, 0)),
            pl.BlockSpec((1, past, w), lambda b, h: (b, 0, h)),
            pl.BlockSpec((t, w), lambda b, h: (b, h)),
            pl.BlockSpec((t, w), lambda b, h: (b, h)),
            pl.BlockSpec((t, w), lambda b, h: (b, h)),
            pl.BlockSpec((2 * hps, t, past), lambda b, h: (h, 0, 0)),
            pl.BlockSpec((2 * hps, t, t), lambda b, h: (h, 0, 0)),
            pl.BlockSpec((1, HEAD_SLAB), lambda b, h: (0, 0)),
        ],
        out_specs=pl.BlockSpec((t, w), lambda b, h: (b, h)),
        out_shape=jax.ShapeDtypeStruct((tt, width), BF16),
        compiler_params=_cparams(("parallel", "parallel")),
        name="da_sample",
    )(lam, q, ckt, cv, kn, vn, sg, bias_c, bias_n, subln.reshape(1, HEAD_SLAB))


def _suffix_matrix(n):
    j = np.arange(n)[:, None]
    s = np.arange(n)[None, :]
    return jnp.asarray((j > s).astype(np.float32), dtype=BF16)


def _sb_logs(z, mask):
    lg = jnp.log(1.0 + jnp.exp2(-jnp.abs(z))) * LOG2E
    ls = jnp.minimum(z, 0.0) - lg
    lr = ls - z
    if mask is not None:
        lr = jnp.where(mask, lr, 0.0)
    return ls, lr


def _sb_suffix(lr, tmat):
    hi = lr.astype(BF16)
    lo = (lr - hi.astype(F32)).astype(BF16)
    return (jnp.dot(hi, tmat, preferred_element_type=F32)
            + jnp.dot(lo, tmat, preferred_element_type=F32))


def _sb_exp(ls, after, c, mask):
    arg = ls + after
    if c is not None:
        arg = arg + c
    a = jnp.exp2(arg)
    if mask is not None:
        a = jnp.where(mask, a, 0.0)
    return a.astype(BF16)


SB_DEAD_LOG2 = -160.0


def _sb_prompt_kernel(q_ref, kt_ref, vt_ref, sg_ref, tmat_ref, o_ref, acc_scr, rem_scr, *, tq, nq, hps):
    tmat = tmat_ref[...]
    r = lax.broadcasted_iota(jnp.int32, (2 * tq, tq), 0) % tq
    c = lax.broadcasted_iota(jnp.int32, (2 * tq, tq), 1)
    qqs = [_split_halves(q_ref[:, hh * HEAD_SLAB:(hh + 1) * HEAD_SLAB]) for hh in range(hps)]

    def body(qv):
        starts = [w * tq for w in range(qv, -1, -1)]

        def scores(hh, w):
            lanes = slice(hh * HEAD_SLAB, (hh + 1) * HEAD_SLAB)
            return jnp.dot(qqs[hh], kt_ref[0, lanes, starts[w]:starts[w] + tq], preferred_element_type=F32)

        def finish(pending, acc):
            ls, after, rem, mask, hh, w = pending
            lanes = slice(hh * HEAD_SLAB, (hh + 1) * HEAD_SLAB)
            a = _sb_exp(ls, after, rem, mask)
            pv = lax.dot_general(a, vt_ref[0, lanes, starts[w]:starts[w] + tq], _NT, preferred_element_type=F32)
            return pv if acc is None else acc + pv

        def run(ws, accs, rems):
            work = [(hh, w) for w in ws for hh in range(hps)]
            ahead = 2 * hps
            zs = {i: scores(*work[i]) for i in range(min(ahead, len(work)))}
            accs, rems = list(accs), list(rems)
            pending = None
            for i, (hh, w) in enumerate(work):
                mask = (c < r) if w == 0 else None
                ls, lr = _sb_logs(zs.pop(i), mask)
                if i + ahead < len(work):
                    zs[i + ahead] = scores(*work[i + ahead])
                if pending is not None:
                    accs[pending[4]] = finish(pending, accs[pending[4]])
                pending = (ls, _sb_suffix(lr, tmat), rems[hh], mask, hh, w)
                tot = jnp.sum(lr, axis=-1, keepdims=True)
                rems[hh] = tot if rems[hh] is None else rems[hh] + tot
            accs[pending[4]] = finish(pending, accs[pending[4]])
            return accs, rems

        ws = list(range(len(starts)))
        accs, rems = run(ws[:2], [None] * hps, [None] * hps)
        if len(ws) > 2:
            alive = None
            for hh in range(hps):
                acc_scr[hh] = accs[hh]
                rem_scr[hh] = rems[hh]
                mx = jnp.max(rems[hh])
                alive = mx if alive is None else jnp.maximum(alive, mx)

            @pl.when(alive > SB_DEAD_LOG2)
            def _():
                accs2, _ = run(ws[2:], [acc_scr[hh] for hh in range(hps)], [rem_scr[hh] for hh in range(hps)])
                for hh in range(hps):
                    acc_scr[hh] = accs2[hh]

            accs = [acc_scr[hh] for hh in range(hps)]
        for hh in range(hps):
            lanes = slice(hh * HEAD_SLAB, (hh + 1) * HEAD_SLAB)
            o_ref[:, lanes] = (sg_ref[:, lanes].astype(F32) * _join_halves(accs[hh], tq)).astype(BF16)

    _per_q_block(pl.program_id(2), nq, body)


def _sb_prompt(q, kt, vt, sg, batch, seq, tq):
    t, width = q.shape
    hps = 2
    nh = width // HEAD_SLAB // hps
    w = hps * HEAD_SLAB
    nq = seq // tq
    return pl.pallas_call(
        functools.partial(_sb_prompt_kernel, tq=tq, nq=nq, hps=hps),
        grid=(batch, nh, nq),
        in_specs=[
            pl.BlockSpec((tq, w), lambda b, h, i: (b * nq + i, h)),
            pl.BlockSpec((1, w, seq), lambda b, h, i: (b, h, 0)),
            pl.BlockSpec((1, w, seq), lambda b, h, i: (b, h, 0)),
            pl.BlockSpec((tq, w), lambda b, h, i: (b * nq + i, h)),
            pl.BlockSpec((tq, tq), lambda b, h, i: (0, 0)),
        ],
        out_specs=pl.BlockSpec((tq, w), lambda b, h, i: (b * nq + i, h)),
        out_shape=jax.ShapeDtypeStruct((t, width), BF16),
        scratch_shapes=[pltpu.VMEM((hps, 2 * tq, HEAD_SLAB), F32), pltpu.VMEM((hps, 2 * tq, 1), F32)],
        compiler_params=_cparams(("parallel", "parallel", "arbitrary")),
        name="sb_prompt",
    )(q, kt, vt, sg, _suffix_matrix(tq))


def _sb_sample_kernel(q_ref, kct_ref, vct_ref, kn_ref, vn_ref, sg_ref, tmat_ref, o_ref, *, t, tn, tk, past, hps):
    tmat = tmat_ref[...]
    pad = jnp.zeros((tn - t, HEAD_SLAB), BF16)
    r = lax.broadcasted_iota(jnp.int32, (2 * t, tn), 0) % t
    c = lax.broadcasted_iota(jnp.int32, (2 * t, tn), 1)
    blocks = []
    for hh in range(hps):
        lanes = slice(hh * HEAD_SLAB, (hh + 1) * HEAD_SLAB)
        qq = _split_halves(q_ref[:, lanes])
        kn = jnp.concatenate([kn_ref[:, lanes], pad], axis=0)
        z = lax.dot_general(qq, kn, _NT, preferred_element_type=F32)
        blocks.append(dict(hh=hh, z=z, mask=c < r, tm=tmat[:tn, :tn], j=None))
        for j in reversed(range(past // tk)):
            z = jnp.dot(qq, kct_ref[0, lanes, j * tk:(j + 1) * tk].astype(BF16), preferred_element_type=F32)
            blocks.append(dict(hh=hh, z=z, mask=None, tm=tmat, j=j))
    rem = [None] * hps
    for bk in blocks:
        bk["ls"], bk["lr"] = _sb_logs(bk.pop("z"), bk["mask"])
        bk["rem"] = rem[bk["hh"]]
        tot = jnp.sum(bk["lr"], axis=-1, keepdims=True)
        rem[bk["hh"]] = tot if bk["rem"] is None else bk["rem"] + tot
    for bk in blocks:
        bk["after"] = _sb_suffix(bk.pop("lr"), bk["tm"])
    acc = [None] * hps
    for bk in blocks:
        hh, j = bk["hh"], bk["j"]
        lanes = slice(hh * HEAD_SLAB, (hh + 1) * HEAD_SLAB)
        a = _sb_exp(bk["ls"], bk["after"], bk["rem"], bk["mask"])
        if j is None:
            vn = jnp.concatenate([vn_ref[:, lanes], pad], axis=0)
            pv = jnp.dot(a, vn, preferred_element_type=F32)
        else:
            vj = vct_ref[0, lanes, j * tk:(j + 1) * tk].astype(BF16)
            pv = lax.dot_general(a, vj, _NT, preferred_element_type=F32)
        acc[hh] = pv if acc[hh] is None else acc[hh] + pv
    for hh in range(hps):
        lanes = slice(hh * HEAD_SLAB, (hh + 1) * HEAD_SLAB)
        o_ref[:, lanes] = (sg_ref[:, lanes].astype(F32) * _join_halves(acc[hh], t)).astype(BF16)


def _sb_sample(q, ckt, cvt, kn, vn, sg, batch, t, tk):
    tt, width = q.shape
    hps = 4
    nh = width // HEAD_SLAB // hps
    w = hps * HEAD_SLAB
    past = ckt.shape[2]
    tk = min(tk, past)
    assert past % tk == 0
    tn = HEAD_SLAB
    assert t <= tn <= tk
    return pl.pallas_call(
        functools.partial(_sb_sample_kernel, t=t, tn=tn, tk=tk, past=past, hps=hps),
        grid=(batch, nh),
        in_specs=[
            pl.BlockSpec((t, w), lambda b, h: (b, h)),
            pl.BlockSpec((1, w, past), lambda b, h: (b, h, 0)),
            pl.BlockSpec((1, w, past), lambda b, h: (b, h, 0)),
            pl.BlockSpec((t, w), lambda b, h: (b, h)),
            pl.BlockSpec((t, w), lambda b, h: (b, h)),
            pl.BlockSpec((t, w), lambda b, h: (b, h)),
            pl.BlockSpec((tk, tk), lambda b, h: (0, 0)),
        ],
        out_specs=pl.BlockSpec((t, w), lambda b, h: (b, h)),
        out_shape=jax.ShapeDtypeStruct((tt, width), BF16),
        compiler_params=_cparams(("parallel", "parallel")),
        name="sb_sample",
    )(q, ckt, cvt, kn, vn, sg, _suffix_matrix(tk))


def _sw_group(sink_ref, q_slabs, kk, vv, bias, valid, kvh, group, t):
    qst = jnp.concatenate([_split_halves(qs) for qs in q_slabs], axis=0)
    return _sw_attend(sink_ref, _sw_scores(qst, kk, bias, valid), vv, kvh, group, t)


def _sw_scores(qst, kk, bias, valid):
    s = lax.dot_general(qst, kk, _NT, preferred_element_type=F32) + bias
    if valid is not None:
        s = jnp.where(valid, s, NEG_INF)
    return s


def _sw_attend(sink_ref, s, vv, kvh, group, t):
    row = lax.broadcasted_iota(jnp.int32, (group * t, 1), 0)
    sk = jnp.zeros((group * t, 1), F32)
    for g in range(group):
        sk = jnp.where(row // t == g, sink_ref[kvh * group + g] * LOG2E, sk)
    m = jnp.maximum(jnp.max(s, axis=-1, keepdims=True), sk)
    p = jnp.exp2(s - m)
    w = p / (jnp.sum(p, axis=-1, keepdims=True) + jnp.exp2(sk - m))
    o = jnp.dot(w.astype(BF16), vv, preferred_element_type=F32)
    return [_join_halves(o[2 * i * t:(2 * i + 2) * t], t) for i in range(group // 2)]


def _sw_prompt_kernel(sink_ref, q_ref, kp_ref, kc_ref, vp_ref, vc_ref, sg_ref, bias_ref, o_ref,
                      *, tq, n_kv, group):
    band = (WIN_CHUNKS + 1) * CHUNK
    kcat = jnp.concatenate([kp_ref[...], kc_ref[...]], axis=0)
    vcat = jnp.concatenate([vp_ref[...], vc_ref[...]], axis=0)
    col = lax.broadcasted_iota(jnp.int32, (1, band), 1)

    def body(first):
        work = [(cc * CHUNK, kvh) for cc in range(tq // CHUNK) for kvh in range(n_kv)]

        def scores(r0, kvh):
            valid = (col >= WINDOW - r0) if first and r0 < WINDOW else None
            s0 = kvh * (group // 2)
            qst = jnp.concatenate(
                [_split_halves(q_ref[r0:r0 + CHUNK, (s0 + i) * HEAD_SLAB:(s0 + i + 1) * HEAD_SLAB])
                 for i in range(group // 2)], axis=0)
            bias = bias_ref[kvh * group:(kvh + 1) * group].reshape(group * CHUNK, band)
            return _sw_scores(qst, kcat[r0:r0 + band, kvh * HEAD_SLAB:(kvh + 1) * HEAD_SLAB], bias, valid)

        ahead = 2
        sc = {i: scores(*work[i]) for i in range(min(ahead, len(work)))}
        for i, (r0, kvh) in enumerate(work):
            s = sc.pop(i)
            if i + ahead < len(work):
                sc[i + ahead] = scores(*work[i + ahead])
            vv = vcat[r0:r0 + band, kvh * HEAD_SLAB:(kvh + 1) * HEAD_SLAB]
            s0 = kvh * (group // 2)
            for j, o in enumerate(_sw_attend(sink_ref, s, vv, kvh, group, CHUNK)):
                lanes = slice((s0 + j) * HEAD_SLAB, (s0 + j + 1) * HEAD_SLAB)
                sg = sg_ref[r0:r0 + CHUNK, lanes].astype(F32)
                o_ref[r0:r0 + CHUNK, lanes] = (sg * o).astype(BF16)

    qi = pl.program_id(1)
    pl.when(qi == 0)(functools.partial(body, True))
    pl.when(qi > 0)(functools.partial(body, False))


def _sw_prompt(q, kd, vd, sg, bias, sinks, batch, seq, tq, n_kv, group):
    t, width = q.shape
    nq = seq // tq
    wpb = tq // WINDOW
    kvw = kd.shape[1]
    band = (WIN_CHUNKS + 1) * CHUNK
    prev = lambda b, i: (jnp.maximum((b * nq + i) * wpb - 1, 0), 0)
    cur = lambda b, i: (b * nq + i, 0)
    return pl.pallas_call(
        functools.partial(_sw_prompt_kernel, tq=tq, n_kv=n_kv, group=group),
        grid=(batch, nq),
        in_specs=[
            pl.BlockSpec(memory_space=pltpu.SMEM),
            pl.BlockSpec((tq, width), cur),
            pl.BlockSpec((WINDOW, kvw), prev),
            pl.BlockSpec((tq, kvw), cur),
            pl.BlockSpec((WINDOW, kvw), prev),
            pl.BlockSpec((tq, kvw), cur),
            pl.BlockSpec((tq, width), cur),
            pl.BlockSpec((n_kv * group, CHUNK, band), lambda b, i: (0, 0, 0)),
        ],
        out_specs=pl.BlockSpec((tq, width), cur),
        out_shape=jax.ShapeDtypeStruct((t, width), BF16),
        compiler_params=_cparams(("parallel", "arbitrary")),
        name="sw_prompt",
    )(sinks, q, kd, kd, vd, vd, sg, bias)


def _sw_sample_kernel(sink_ref, q_ref, k_ref, v_ref, sg_ref, bias_ref, o_ref, *, t, past, wb, n_kv, group):
    band = wb + t
    qpos = past + lax.broadcasted_iota(jnp.int32, (group * t, 1), 0) % t
    kpos = past - wb + lax.broadcasted_iota(jnp.int32, (1, band), 1)
    qc = qpos // CHUNK
    kc = kpos // CHUNK
    valid = (kc <= qc) & (kc >= qc - WIN_CHUNKS)
    for kvh in range(n_kv):
        kk = k_ref[0, :, kvh * HEAD_SLAB:(kvh + 1) * HEAD_SLAB]
        vv = v_ref[0, :, kvh * HEAD_SLAB:(kvh + 1) * HEAD_SLAB]
        s0 = kvh * (group // 2)
        q_slabs = [q_ref[:, (s0 + i) * HEAD_SLAB:(s0 + i + 1) * HEAD_SLAB] for i in range(group // 2)]
        bias = bias_ref[kvh * group:(kvh + 1) * group].reshape(group * t, band)
        outs = _sw_group(sink_ref, q_slabs, kk, vv, bias, valid, kvh, group, t)
        for i, o in enumerate(outs):
            lanes = slice((s0 + i) * HEAD_SLAB, (s0 + i + 1) * HEAD_SLAB)
            o_ref[:, lanes] = (sg_ref[:, lanes].astype(F32) * o).astype(BF16)


def _sw_sample(q, kd, vd, sg, bias, sinks, batch, t, past, wb, n_kv, group):
    tt, width = q.shape
    band, kvw = kd.shape[1:]
    return pl.pallas_call(
        functools.partial(_sw_sample_kernel, t=t, past=past, wb=wb, n_kv=n_kv, group=group),
        grid=(batch,),
        in_specs=[
            pl.BlockSpec(memory_space=pltpu.SMEM),
            pl.BlockSpec((t, width), lambda b: (b, 0)),
            pl.BlockSpec((1, band, kvw), lambda b: (b, 0, 0)),
            pl.BlockSpec((1, band, kvw), lambda b: (b, 0, 0)),
            pl.BlockSpec((t, width), lambda b: (b, 0)),
            pl.BlockSpec((n_kv * group, t, band), lambda b: (0, 0, 0)),
        ],
        out_specs=pl.BlockSpec((t, width), lambda b: (b, 0)),
        out_shape=jax.ShapeDtypeStruct((tt, width), BF16),
        compiler_params=_cparams(("parallel",)),
        name="sw_sample",
    )(sinks, q, kd, vd, sg, bias)


TM_PROJ = 512
TQ_ATTN = 256


def _feature_major(a):
    nd = a.ndim
    a = jnp.transpose(a, (0,) + tuple(range(2, nd)) + (1,))
    return a.reshape(a.shape[0], -1, a.shape[-1])


def _position_major(a, feat_shape):
    b, _, p = a.shape
    a = a.reshape((b,) + tuple(feat_shape) + (p,))
    nd = a.ndim
    return jnp.transpose(a, (0, nd - 1) + tuple(range(1, nd - 1)))


def _project(x, res, norm, w_nat, plan, dtypes, tm, **kw):
    outs = _inproj(x, norm, w_nat, plan, dtypes, tm, res=res, **kw)
    return (x, outs) if res is None else (outs[0], outs[1:])


def _da_layer(xp, xs, res_p, res_s, ck, cv, table, norm, w_in, lam_params, subln, lam_init, dims):
    batch, seq, dec_batch, t_dec, past, d = dims
    br = d
    wq, wk, wv, wg = (w_in[:, i * br:(i + 1) * br] for i in range(4))
    w_p = jnp.concatenate([wq, wv, wg], axis=1).astype(BF16)
    plan_p = [(0, br, "qscale2"), (br, br, "none"), (2 * br, br, "silu"), (br, br, "none")]
    xp, (qp, vp, sgp, vpf, ktf, kt) = _project(
        xp, res_p, norm, w_p, plan_p, [BF16, BF16, BF16, F32], TM_PROJ,
        w_t=wk.T.astype(BF16), plan_t=[(0, br), (0, br)], dtypes_t=[F32, BF16], seq=seq)
    plan_s = [(0, br, "qscale2"), (br, br, "none"), (2 * br, br, "none"), (3 * br, br, "silu"),
              (br, br, "none"), (2 * br, br, "none")]
    xs, (qs, ks, vs, sgs, ksf, vsf) = _project(xs, res_s, norm, w_in.astype(BF16), plan_s,
                                               [BF16, BF16, BF16, BF16, F32, F32], TM_PROJ)
    lam = _diff_lambda(lam_params, lam_init)
    tq = min(TQ_ATTN, seq)
    assert seq % tq == 0 and tq >= MAX_DISTANCE and tq % CHUNK == 0
    assert past == ck.shape[1] and past % CHUNK == 0 and t_dec <= CHUNK
    i = jnp.arange(tq)[:, None]
    j = jnp.arange(tq)[None, :]
    rel_p = jnp.concatenate([j - i, j - tq - i], axis=0)
    nbh = table.shape[1]
    bias_p = _bias_tiles(table, rel_p, LOG2E).reshape(nbh, 2, tq, tq)
    og_p = _da_prompt(qp, kt, vp, sgp, bias_p, table, lam, subln, batch, seq, tq, lam_init)
    q_pos = past + jnp.arange(t_dec)[:, None]
    bias_c = _bias_tiles(table, jnp.arange(past)[None, :] - q_pos, LOG2E)
    bias_n = _bias_tiles(table, past + jnp.arange(t_dec)[None, :] - q_pos, LOG2E)
    og_s = _da_sample(qs, _feature_major(ck), cv.reshape(dec_batch, past, br), ks, vs, sgs,
                      bias_c, bias_n, lam, subln, dec_batch, t_dec, lam_init)
    state = (_position_major(ktf, ck.shape[2:]), vpf.reshape((batch, seq) + cv.shape[2:]),
             ksf.reshape((dec_batch, t_dec) + ck.shape[2:]), vsf.reshape((dec_batch, t_dec) + cv.shape[2:]))
    return xp, xs, og_p, og_s, state


def _sb_layer(xp, xs, res_p, res_s, ck, cv, norm, w_in, dims):
    batch, seq, dec_batch, t_dec, past, d = dims
    br = d
    wq, wk, wv, wg = (w_in[:, i * br:(i + 1) * br] for i in range(4))
    w_p = jnp.concatenate([wq, wg], axis=1).astype(BF16)
    w_t = jnp.concatenate([wk, wv], axis=1).T.astype(BF16)
    xp, (qp, sgp, ktf, kt, vtf, vt) = _project(
        xp, res_p, norm, w_p, [(0, br, "qscale2"), (br, br, "silu")], [BF16, BF16], TM_PROJ,
        w_t=w_t, plan_t=[(0, br), (0, br), (br, br), (br, br)], dtypes_t=[F32, BF16, F32, BF16], seq=seq)
    plan_s = [(0, br, "qscale2"), (br, br, "none"), (2 * br, br, "none"), (3 * br, br, "silu"),
              (br, br, "none"), (2 * br, br, "none")]
    xs, (qs, ks, vs, sgs, ksf, vsf) = _project(xs, res_s, norm, w_in.astype(BF16), plan_s,
                                               [BF16, BF16, BF16, BF16, F32, F32], TM_PROJ)
    tq = min(TQ_ATTN, seq)
    assert seq % tq == 0 and past == ck.shape[1]
    og_p = _sb_prompt(qp, kt, vt, sgp, batch, seq, tq)
    og_s = _sb_sample(qs, _feature_major(ck), _feature_major(cv), ks, vs, sgs, dec_batch, t_dec, TQ_ATTN)
    state = (_position_major(ktf, ck.shape[2:]), _position_major(vtf, cv.shape[2:]),
             ksf.reshape((dec_batch, t_dec) + ck.shape[2:]), vsf.reshape((dec_batch, t_dec) + cv.shape[2:]))
    return xp, xs, og_p, og_s, state


def _dup_heads(a, n_kv, hd):
    lead = a.shape[:-1]
    a = a.reshape(lead + (n_kv, 1, hd))
    return jnp.broadcast_to(a, lead + (n_kv, 2, hd)).reshape(lead + (n_kv * 2 * hd,))


def _sw_layer(xp, xs, res_p, res_s, ck, cv, table, norm, w_in, sinks, dims):
    batch, seq, dec_batch, t_dec, past, d = dims
    br = d
    wb, n_kv, hd = ck.shape[1:]
    assert hd == HALF
    kvw = n_kv * hd
    group = br // hd // n_kv
    assert group % 2 == 0
    wq, wk, wv, wg = (w_in[:, :br], w_in[:, br:br + kvw], w_in[:, br + kvw:br + 2 * kvw],
                      w_in[:, br + 2 * kvw:])
    w_p = jnp.concatenate([wq, _dup_heads(wk, n_kv, hd), _dup_heads(wv, n_kv, hd), wg], axis=1).astype(BF16)
    plan_p = [(0, br, "qscale2"), (br, 2 * kvw, "none"), (br + 2 * kvw, 2 * kvw, "none"),
              (br + 4 * kvw, br, "silu")]
    xp, (qp, kdp, vdp, sgp) = _project(xp, res_p, norm, w_p, plan_p, [BF16] * 4, TM_PROJ)
    keep = min(WINDOW, seq)
    w_kv_t = jnp.concatenate([wk, wv], axis=1).T.astype(BF16)
    x_tail = xp.reshape(batch, seq, d)[:, seq - keep:].reshape(batch * keep, d)
    ktf, vtf = _inproj(x_tail, norm, None, [], [], keep, w_t=w_kv_t,
                       plan_t=[(0, kvw), (kvw, kvw)], dtypes_t=[F32, F32], seq=keep)
    plan_s = [(0, br, "qscale2"), (br, kvw, "none"), (br + kvw, kvw, "none"), (br + 2 * kvw, br, "silu")]
    xs, (qs, ksf, vsf, sgs) = _project(xs, res_s, norm, w_in.astype(BF16), plan_s, [BF16, F32, F32, BF16],
                                       TM_PROJ)

    tq = min(TQ_ATTN, seq)
    assert seq % tq == 0 and tq % WINDOW == 0
    band = (WIN_CHUNKS + 1) * CHUNK
    rel_p = (jnp.arange(band)[None, :] - WIN_CHUNKS * CHUNK) - jnp.arange(CHUNK)[:, None]
    bias_p = _bias_tiles(table, rel_p, LOG2E)
    og_p = _sw_prompt(qp, kdp, vdp, sgp, bias_p, sinks, batch, seq, tq, n_kv, group)

    k_all = jnp.concatenate([ck, ksf.reshape(dec_batch, t_dec, n_kv, hd)], axis=1)
    v_all = jnp.concatenate([cv, vsf.reshape(dec_batch, t_dec, n_kv, hd)], axis=1)
    kd_s = _dup_heads(k_all.reshape(dec_batch, wb + t_dec, kvw), n_kv, hd).astype(BF16)
    vd_s = _dup_heads(v_all.reshape(dec_batch, wb + t_dec, kvw), n_kv, hd).astype(BF16)
    rel_s = (past - wb + jnp.arange(wb + t_dec)[None, :]) - (past + jnp.arange(t_dec)[:, None])
    bias_s = _bias_tiles(table, rel_s, LOG2E)
    og_s = _sw_sample(qs, kd_s, vd_s, sgs, bias_s, sinks, dec_batch, t_dec, past, wb, n_kv, group)
    state = (_position_major(ktf, (n_kv, hd)), _position_major(vtf, (n_kv, hd)),
             k_all[:, -wb:], v_all[:, -wb:])
    return xp, xs, og_p, og_s, state


def kernel(x_prompt, x_sample, cache_k_0, cache_v_0, cache_k_1, cache_v_1, cache_k_2, cache_v_2,
           cache_k_3, cache_v_3, rel_bias_table,
           norm_0, w_in_0, w_out_0, da_lambda_0, da_subln_0,
           norm_1, w_in_1, w_out_1,
           norm_2, w_in_2, w_out_2, sw_sinks_2,
           norm_3, w_in_3, w_out_3, da_lambda_3, da_subln_3,
           final_norm):
    batch, seq, d = x_prompt.shape
    dec_batch, t_dec, _ = x_sample.shape
    dims = (batch, seq, dec_batch, t_dec, cache_k_0.shape[1], d)
    xp = x_prompt.reshape(batch * seq, d)
    xs = x_sample.reshape(dec_batch * t_dec, d)
    layers = [
        ("da", cache_k_0, cache_v_0, norm_0, w_in_0, w_out_0, (da_lambda_0, da_subln_0)),
        ("sb", cache_k_1, cache_v_1, norm_1, w_in_1, w_out_1, ()),
        ("sw", cache_k_2, cache_v_2, norm_2, w_in_2, w_out_2, (sw_sinks_2,)),
        ("da", cache_k_3, cache_v_3, norm_3, w_in_3, w_out_3, (da_lambda_3, da_subln_3)),
    ]
    states = []
    res_p = res_s = None
    for i, (kind, ck, cv, norm, w_in, w_out, extra) in enumerate(layers):
        if kind == "da":
            lam_init = 0.8 - 0.6 * math.exp(-0.3 * i)
            xp, xs, og_p, og_s, st = _da_layer(xp, xs, res_p, res_s, ck, cv, rel_bias_table, norm, w_in,
                                               extra[0], extra[1], lam_init, dims)
        elif kind == "sb":
            xp, xs, og_p, og_s, st = _sb_layer(xp, xs, res_p, res_s, ck, cv, norm, w_in, dims)
        else:
            xp, xs, og_p, og_s, st = _sw_layer(xp, xs, res_p, res_s, ck, cv, rel_bias_table, norm, w_in,
                                               extra[0], dims)
        w_o = w_out.astype(BF16)
        res_p, res_s = (og_p, w_o), (og_s, w_o)
        states.append(st)
    xp = _outproj(xp, res_p[0], res_p[1], final_norm, 2 * TM_PROJ)
    xs = _outproj(xs, res_s[0], res_s[1], final_norm, 2 * TM_PROJ)
    out = [xp.reshape(batch, seq, d), xs.reshape(dec_batch, t_dec, d)]
    for st in states:
        out.extend(st)
    return tuple(out)
```
